```python
import jax, jax.numpy as jnp
from jax import lax
import numpy as np

D_MODEL = 2048
BATCH = 1
SEQ = 8192
DEPTH = 4

MIX_WIDTH = 2 * D_MODEL
SSD_HEAD_DIM = 64
SSD_WIDTH = 3 * MIX_WIDTH // 4
SSD_HEADS = SSD_WIDTH // SSD_HEAD_DIM
SSD_GROUPS = 8
SSD_STATE = 128
SSD_CHUNK = 256
SSD_CONV = 5
SSD_CONV_CH = SSD_WIDTH + 2 * SSD_GROUPS * SSD_STATE
FOURIER_WIDTH = MIX_WIDTH - SSD_WIDTH
FOURIER_GROUPS = 4
FOURIER_GROUP_DIM = FOURIER_WIDTH // FOURIER_GROUPS
EVEN_IN = SSD_WIDTH + SSD_CONV_CH + 2 * SSD_HEADS + FOURIER_WIDTH
SGU_WIDTH = MIX_WIDTH
SGU_GROUPS = 8
SGU_GROUP_DIM = SGU_WIDTH // SGU_GROUPS
SGU_CHUNK = 128
D_FF = 5632
N_EVEN = (DEPTH + 1) // 2
N_ODD = DEPTH // 2
EPS = 1e-6

kernel_name = "macaron_ssd_fourier_sgu_encoder"


def group_rmsnorm(x, g, groups=1):
    shp = x.shape
    xf = x.astype(jnp.float32).reshape(shp[:-1] + (groups, shp[-1] // groups))
    xf = xf * lax.rsqrt(jnp.mean(xf * xf, axis=-1, keepdims=True) + EPS)
    return (xf.reshape(shp) * g.astype(jnp.float32)).astype(x.dtype)


def swiglu(x, w_gate, w_up, w_down):
    return (jax.nn.silu(x @ w_gate) * (x @ w_up)) @ w_down


def centred_dwconv(x, w, b):
    k = w.shape[0]
    y = lax.conv_general_dilated(x, w[:, None, :], window_strides=(1,), padding=[(k // 2, k // 2)],
                                 dimension_numbers=('NWC', 'WIO', 'NWC'), feature_group_count=x.shape[-1])
    return y + b


def segsum(a):
    cs = jnp.cumsum(a, axis=-1)
    d = cs[..., :, None] - cs[..., None, :]
    t = a.shape[-1]
    mask = jnp.tril(jnp.ones((t, t), dtype=bool))
    return jnp.where(mask, d, -jnp.inf)


def ssd_chunked(x, dt, a, bm, cm):
    bsz, seqlen, nh, hp = x.shape
    ng, ns = bm.shape[-2:]
    nr = nh // ng
    pad = (-seqlen) % SSD_CHUNK
    if pad:
        padw = lambda t: [(0, 0), (0, pad)] + [(0, 0)] * (t.ndim - 2)
        x, dt, bm, cm = (jnp.pad(t, padw(t)) for t in (x, dt, bm, cm))
    lp = seqlen + pad
    nc, tc = lp // SSD_CHUNK, SSD_CHUNK
    xdt = (x * dt[..., None]).reshape(bsz, nc, tc, ng, nr, hp)
    da = jnp.moveaxis((dt * a).reshape(bsz, nc, tc, ng, nr), 2, -1)
    bm = bm.reshape(bsz, nc, tc, ng, ns)
    cm = cm.reshape(bsz, nc, tc, ng, ns)
    a_cs = jnp.cumsum(da, axis=-1)
    decay = jnp.exp(segsum(da))
    cb = jnp.einsum('bctgn,bcsgn->bcgts', cm, bm)
    y_diag = jnp.einsum('bcgrts,bcsgrp->bctgrp', cb[:, :, :, None] * decay, xdt)
    decay_states = jnp.exp(a_cs[..., -1:] - a_cs)
    states = jnp.einsum('bctgn,bcgrt,bctgrp->bcgrpn', bm, decay_states, xdt)
    chunk_a = jnp.pad(jnp.moveaxis(a_cs[..., -1], 1, -1), [(0, 0)] * 3 + [(1, 0)])
    decay_chunk = jnp.exp(segsum(chunk_a))
    states = jnp.pad(states, [(0, 0), (1, 0)] + [(0, 0)] * 4)
    prev_states = jnp.einsum('bgrzc,bcgrpn->bzgrpn', decay_chunk, states)[:, :-1]
    y_off = jnp.einsum('bctgn,bcgrpn,bcgrt->bctgrp', cm, prev_states, jnp.exp(a_cs))
    y = (y_diag + y_off).reshape(bsz, lp, nh, hp)
    return y[:, :seqlen]


def ssd_fourier_mixer(h, w_in, conv_w, conv_b, dt_bias, a_log, d_skip, ssd_norm, fourier_w, w_out):
    f32 = jnp.float32
    bsz, seqlen, _ = h.shape
    proj = h @ w_in
    z, xbc, dt_raw, u = jnp.split(proj, [SSD_WIDTH, SSD_WIDTH + SSD_CONV_CH,
                                         SSD_WIDTH + SSD_CONV_CH + 2 * SSD_HEADS], axis=-1)
    xbc = jax.nn.silu(centred_dwconv(xbc, conv_w, conv_b))
    xs, bm, cm = jnp.split(xbc, [SSD_WIDTH, SSD_WIDTH + SSD_GROUPS * SSD_STATE], axis=-1)
    xs = xs.reshape(bsz, seqlen, SSD_HEADS, SSD_HEAD_DIM).astype(f32)
    bm = bm.reshape(bsz, seqlen, SSD_GROUPS, SSD_STATE).astype(f32)
    cm = cm.reshape(bsz, seqlen, SSD_GROUPS, SSD_STATE).astype(f32)
    dt = jax.nn.softplus(dt_raw.astype(f32).reshape(bsz, seqlen, 2, SSD_HEADS) + dt_bias.astype(f32))
    a = -jnp.exp(a_log.astype(f32))
    y_fwd = ssd_chunked(xs, dt[:, :, 0], a[0], bm, cm)
    y_bwd = jnp.flip(ssd_chunked(jnp.flip(xs, 1), jnp.flip(dt[:, :, 1], 1), a[1],
                                 jnp.flip(bm, 1), jnp.flip(cm, 1)), 1)
    y = y_fwd + y_bwd + d_skip.astype(f32)[:, None] * xs
    y = y.reshape(bsz, seqlen, SSD_WIDTH).astype(h.dtype) * jax.nn.silu(z)
    y_ssd = group_rmsnorm(y, ssd_norm, SSD_GROUPS)
    uf = u.astype(f32).reshape(bsz, seqlen, FOURIER_GROUPS, FOURIER_GROUP_DIM)
    fre = jnp.fft.fft2(uf, axes=(1, 3), norm='ortho').real
    y_fft = jnp.einsum('blgd,gde->blge', fre, fourier_w.astype(f32))
    y_fft = y_fft.reshape(bsz, seqlen, FOURIER_WIDTH).astype(h.dtype)
    return jnp.concatenate([y_ssd, y_fft], axis=-1) @ w_out


def spatial_gating_mixer(h, w_uv, b_uv, sgu_norm, w_s, b_s, w_out):
    bsz, seqlen, _ = h.shape
    uv = jax.nn.gelu(h @ w_uv + b_uv)
    u, v = jnp.split(uv, 2, axis=-1)
    v = group_rmsnorm(v, sgu_norm)
    v = v.reshape(bsz, seqlen // SGU_CHUNK, SGU_CHUNK, SGU_GROUPS, SGU_GROUP_DIM)
    v = jnp.einsum('gts,bcsgd->bctgd', w_s, v) + b_s.T[:, :, None]
    return (u * v.reshape(bsz, seqlen, SGU_WIDTH)) @ w_out


def setup_inputs(seed: int = 0) -> dict:
    key = jax.random.key(seed)
    ks = iter(jax.random.split(key, 40))
    f32 = jnp.float32
    nrm = lambda shape, scale: jax.random.normal(next(ks), shape, f32) * scale
    gain = lambda shape: 1.0 + nrm(shape, 0.02)
    dt0 = jnp.exp(jax.random.uniform(next(ks), (N_EVEN, 2, SSD_HEADS), f32,
                                     minval=np.log(1e-3), maxval=np.log(1e-1)))
    dt_bias = dt0 + jnp.log(-jnp.expm1(-dt0))
    a_log = jnp.log(jax.random.uniform(next(ks), (N_EVEN, 2, SSD_HEADS), f32, minval=1.0, maxval=16.0))
    return {
        'x': nrm((BATCH, SEQ, D_MODEL), 1.0),
        'ffn1_norm': gain((DEPTH, D_MODEL)),
        'ffn1_w_gate': nrm((DEPTH, D_MODEL, D_FF), D_MODEL ** -0.5),
        'ffn1_w_up': nrm((DEPTH, D_MODEL, D_FF), D_MODEL ** -0.5),
        'ffn1_w_down': nrm((DEPTH, D_FF, D_MODEL), D_FF ** -0.5),
        'mix_norm': gain((DEPTH, D_MODEL)),
        'ffn2_norm': gain((DEPTH, D_MODEL)),
        'ffn2_w_gate': nrm((DEPTH, D_MODEL, D_FF), D_MODEL ** -0.5),
        'ffn2_w_up': nrm((DEPTH, D_MODEL, D_FF), D_MODEL ** -0.5),
        'ffn2_w_down': nrm((DEPTH, D_FF, D_MODEL), D_FF ** -0.5),
        'even_w_in': nrm((N_EVEN, D_MODEL, EVEN_IN), D_MODEL ** -0.5),
        'ssd_conv_w': nrm((N_EVEN, SSD_CONV, SSD_CONV_CH), SSD_CONV ** -0.5),
        'ssd_conv_b': nrm((N_EVEN, SSD_CONV_CH), 0.01),
        'ssd_dt_bias': dt_bias,
        'ssd_a_log': a_log,
        'ssd_d': 1.0 + nrm((N_EVEN, SSD_HEADS), 0.1),
        'ssd_norm': gain((N_EVEN, SSD_WIDTH)),
        'fourier_w': nrm((N_EVEN, FOURIER_GROUPS, FOURIER_GROUP_DIM, FOURIER_GROUP_DIM), FOURIER_GROUP_DIM ** -0.5),
        'even_w_out': nrm((N_EVEN, MIX_WIDTH, D_MODEL), MIX_WIDTH ** -0.5),
        'sgu_w_uv': nrm((N_ODD, D_MODEL, 2 * SGU_WIDTH), D_MODEL ** -0.5),
        'sgu_b_uv': nrm((N_ODD, 2 * SGU_WIDTH), 0.01),
        'sgu_norm': gain((N_ODD, SGU_WIDTH)),
        'sgu_w_s': nrm((N_ODD, SGU_GROUPS, SGU_CHUNK, SGU_CHUNK), SGU_CHUNK ** -0.5),
        'sgu_b_s': 1.0 + nrm((N_ODD, SGU_GROUPS, SGU_CHUNK), 0.01),
        'odd_w_out': nrm((N_ODD, SGU_WIDTH, D_MODEL), SGU_WIDTH ** -0.5),
        'final_norm': gain((D_MODEL,)),
    }


def reference(x, ffn1_norm, ffn1_w_gate, ffn1_w_up, ffn1_w_down, mix_norm, ffn2_norm, ffn2_w_gate,
              ffn2_w_up, ffn2_w_down, even_w_in, ssd_conv_w, ssd_conv_b, ssd_dt_bias, ssd_a_log, ssd_d,
              ssd_norm, fourier_w, even_w_out, sgu_w_uv, sgu_b_uv, sgu_norm, sgu_w_s, sgu_b_s, odd_w_out,
              final_norm):
    for i in range(DEPTH):
        x = x + 0.5 * swiglu(group_rmsnorm(x, ffn1_norm[i]), ffn1_w_gate[i], ffn1_w_up[i], ffn1_w_down[i])
        h = group_rmsnorm(x, mix_norm[i])
        j = i // 2
        if i % 2 == 0:
            x = x + ssd_fourier_mixer(h, even_w_in[j], ssd_conv_w[j], ssd_conv_b[j], ssd_dt_bias[j],
                                      ssd_a_log[j], ssd_d[j], ssd_norm[j], fourier_w[j], even_w_out[j])
        else:
            x = x + spatial_gating_mixer(h, sgu_w_uv[j], sgu_b_uv[j], sgu_norm[j], sgu_w_s[j],
                                         sgu_b_s[j], odd_w_out[j])
        x = x + 0.5 * swiglu(group_rmsnorm(x, ffn2_norm[i]), ffn2_w_gate[i], ffn2_w_up[i], ffn2_w_down[i])
    return group_rmsnorm(x, final_norm)
```

```python
import functools

import numpy as np
import jax
import jax.numpy as jnp
from jax import lax
from jax.experimental import pallas as pl
from jax.experimental.pallas import tpu as pltpu

F32 = jnp.float32
BF16 = jnp.bfloat16
HIGHEST = lax.Precision.HIGHEST

D_MODEL = 2048
D_FF = 5632
EPS = 1e-6
SSD_HEAD_DIM = 64
SSD_HEADS = 48
SSD_GROUPS = 8
SSD_RANK = SSD_HEADS // SSD_GROUPS
SSD_GW = SSD_RANK * SSD_HEAD_DIM
SSD_STATE = 128
SSD_CHUNK = 256
SSD_WIDTH = SSD_HEADS * SSD_HEAD_DIM
SSD_CONV = 5
SSD_CONV_CH = SSD_WIDTH + 2 * SSD_GROUPS * SSD_STATE
FOURIER_WIDTH = 1024
FOURIER_GROUPS = 4
FOURIER_GD = FOURIER_WIDTH // FOURIER_GROUPS
SGU_WIDTH = 4096
SGU_GROUPS = 8
SGU_GD = SGU_WIDTH // SGU_GROUPS
SGU_CHUNK = 128
ZX_WIDTH = SSD_WIDTH + SSD_CONV_CH

LANES = 128
SUBLANES = 8
VMEM_LIMIT = 56 * 1024 * 1024

TM = 512
TF = 512
TN = 1024
DFT_N1 = 128
DFT_KB = 8


def _cparams(*sem):
    return pltpu.CompilerParams(dimension_semantics=sem, vmem_limit_bytes=VMEM_LIMIT)


def _rmsnorm(x, w):
    ms = jnp.mean(x * x, axis=-1, keepdims=True)
    return x * lax.rsqrt(ms + EPS) * w


def _ffn_body(x_ref, nw_ref, wg_ref, wu_ref, wd_ref, o_ref, xn_ref):
    j = pl.program_id(1)

    @pl.when(j == 0)
    def _():
        x = x_ref[...]
        xn_ref[...] = _rmsnorm(x, nw_ref[...]).astype(BF16)
        o_ref[...] = x

    xn = xn_ref[...]
    g = jnp.dot(xn, wg_ref[...], preferred_element_type=F32)
    u = jnp.dot(xn, wu_ref[...], preferred_element_type=F32)
    h = ((0.5 * g) * jax.nn.sigmoid(g) * u).astype(BF16)
    o_ref[...] += jnp.dot(h, wd_ref[...], preferred_element_type=F32)


def _ffn(x, nw, wg, wu, wd, layer):
    seq = x.shape[0]
    return pl.pallas_call(
        _ffn_body,
        grid=(seq // TM, D_FF // TF),
        in_specs=[
            pl.BlockSpec((TM, D_MODEL), lambda i, j: (i, 0)),
            pl.BlockSpec((None, 1, D_MODEL), lambda i, j: (layer, 0, 0)),
            pl.BlockSpec((None, D_MODEL, TF), lambda i, j: (layer, 0, j)),
            pl.BlockSpec((None, D_MODEL, TF), lambda i, j: (layer, 0, j)),
            pl.BlockSpec((None, TF, D_MODEL), lambda i, j: (layer, j, 0)),
        ],
        out_specs=pl.BlockSpec((TM, D_MODEL), lambda i, j: (i, 0)),
        out_shape=jax.ShapeDtypeStruct((seq, D_MODEL), F32),
        scratch_shapes=[pltpu.VMEM((TM, D_MODEL), BF16)],
        compiler_params=_cparams("parallel", "arbitrary"),
        name="ffn",
    )(x, nw, wg, wu, wd)


N_ZX_TILES = ZX_WIDTH // TN
IN_TILES = N_ZX_TILES + 2


def _inproj_body(x_ref, nw_ref, w_ref, zx_ref, u_ref, dt_ref, h_ref):
    j = pl.program_id(1)

    @pl.when(j == 0)
    def _():
        h_ref[...] = _rmsnorm(x_ref[...], nw_ref[...]).astype(BF16)

    acc = jnp.dot(h_ref[...], w_ref[...], preferred_element_type=F32)

    @pl.when(j < N_ZX_TILES)
    def _():
        zx_ref[...] = acc.astype(BF16)

    @pl.when(j == N_ZX_TILES)
    def _():
        u_ref[...] = acc.astype(BF16)

    @pl.when(j == N_ZX_TILES + 1)
    def _():
        dt_ref[...] = acc


def _inproj(x, nw, w, layer):
    seq = x.shape[0]
    return pl.pallas_call(
        _inproj_body,
        grid=(seq // TM, IN_TILES),
        in_specs=[
            pl.BlockSpec((TM, D_MODEL), lambda i, j: (i, 0)),
            pl.BlockSpec((None, 1, D_MODEL), lambda i, j: (layer, 0, 0)),
            pl.BlockSpec((D_MODEL, TN), lambda i, j: (0, j)),
        ],
        out_specs=[
            pl.BlockSpec((TM, TN), lambda i, j: (i, jnp.minimum(j, N_ZX_TILES - 1))),
            pl.BlockSpec((TM, FOURIER_WIDTH), lambda i, j: (i, 0)),
            pl.BlockSpec((TM, SSD_GROUPS * LANES), lambda i, j: (i, 0)),
        ],
        out_shape=[
            jax.ShapeDtypeStruct((seq, ZX_WIDTH), BF16),
            jax.ShapeDtypeStruct((seq, FOURIER_WIDTH), BF16),
            jax.ShapeDtypeStruct((seq, SSD_GROUPS * LANES), F32),
        ],
        scratch_shapes=[pltpu.VMEM((TM, D_MODEL), BF16)],
        compiler_params=_cparams("parallel", "arbitrary"),
        name="even_inproj",
    )(x, nw, w)


CONV_TR = 1024
CONV_TC = 512
CONV_HALO = 16


def _conv_body(xm_ref, xp_ref, xn_ref, w_ref, b_ref, o_ref, ext_ref):
    i = pl.program_id(0)
    last = pl.num_programs(0) - 1
    tr = xm_ref.shape[0]
    prev = xp_ref[...].astype(F32)[CONV_HALO - SUBLANES:]
    nxt = xn_ref[...].astype(F32)[:SUBLANES]
    ext_ref[0:SUBLANES, :] = jnp.where(i == 0, 0.0, prev)
    ext_ref[SUBLANES:SUBLANES + tr, :] = xm_ref[...].astype(F32)
    ext_ref[SUBLANES + tr:, :] = jnp.where(i == last, 0.0, nxt)
    w = w_ref[...]
    acc = jnp.broadcast_to(b_ref[...], o_ref.shape)
    half = SSD_CONV // 2
    for k in range(SSD_CONV):
        acc = acc + ext_ref[pl.ds(SUBLANES + k - half, tr), :] * w[k:k + 1, :]
    o_ref[...] = (acc * jax.nn.sigmoid(acc)).astype(BF16)


def _conv(zx, conv_w, conv_b, layer):
    seq = zx.shape[0]
    tr = min(CONV_TR, seq)
    col0 = SSD_WIDTH // CONV_TC
    hb = tr // CONV_HALO
    nhb = seq // CONV_HALO
    return pl.pallas_call(
        _conv_body,
        grid=(seq // tr, SSD_CONV_CH // CONV_TC),
        in_specs=[
            pl.BlockSpec((tr, CONV_TC), lambda i, j: (i, col0 + j)),
            pl.BlockSpec((CONV_HALO, CONV_TC), lambda i, j: (jnp.maximum(i * hb - 1, 0), col0 + j)),
            pl.BlockSpec((CONV_HALO, CONV_TC), lambda i, j: (jnp.minimum((i + 1) * hb, nhb - 1), col0 + j)),
            pl.BlockSpec((None, SSD_CONV, CONV_TC), lambda i, j: (layer, 0, j)),
            pl.BlockSpec((None, 1, CONV_TC), lambda i, j: (layer, 0, j)),
        ],
        out_specs=pl.BlockSpec((tr, CONV_TC), lambda i, j: (i, j)),
        out_shape=jax.ShapeDtypeStruct((seq, SSD_CONV_CH), BF16),
        scratch_shapes=[pltpu.VMEM((tr + 2 * SUBLANES, CONV_TC), F32)],
        compiler_params=_cparams("parallel", "parallel"),
        name="ssd_conv",
    )(zx, zx, zx, conv_w, conv_b)


def _softplus(v):
    return jnp.maximum(v, 0.0) + jnp.log1p(jnp.exp(-jnp.abs(v)))


def _ssd_body(x_ref, b_ref, c_ref, z_ref, dt_ref, par_ref, dskip_ref, nw_ref, e_ref, tri_ref,
              o_ref, sf_ref, sb_ref, sball_ref):
    phase = pl.program_id(1)
    c = pl.program_id(2)
    nc = pl.num_programs(2)
    t = SSD_CHUNK
    gw = SSD_GW

    par = par_ref[...]
    lane = lax.broadcasted_iota(jnp.int32, (1, LANES), 1)
    dt = _softplus(dt_ref[...] + par[0:1, :])
    da = dt * (-jnp.exp(par[1:2, :]))
    cs = jnp.dot(tri_ref[...], da, preferred_element_type=F32, precision=HIGHEST)
    tot = cs[t - 1:t, :]
    q = cs - jnp.where(lane >= SSD_RANK, da, 0.0)
    e = e_ref[...]
    dt_x = jnp.dot(dt, e, preferred_element_type=F32, precision=HIGHEST)
    q_x = jnp.dot(q, e, preferred_element_type=F32, precision=HIGHEST)
    tot_x = jnp.dot(jnp.broadcast_to(tot, (SUBLANES, LANES)), e,
                    preferred_element_type=F32, precision=HIGHEST)[0:1, :]
    x = x_ref[...].astype(F32)
    bm = b_ref[...]

    @pl.when(phase == 0)
    def _():
        @pl.when(c == 0)
        def _():
            sb_ref[...] = jnp.zeros_like(sb_ref)

        cc = nc - 1 - c
        sb = sb_ref[...]
        sball_ref[cc] = sb.astype(BF16)
        w = (x * dt_x[:, gw:] * jnp.exp(q_x[:, gw:])).astype(BF16)
        upd = lax.dot_general(bm, w, (((0,), (0,)), ((), ())), preferred_element_type=F32)
        sb_ref[...] = jnp.exp(tot_x[:, gw:]) * sb + upd

    @pl.when(phase == 1)
    def _():
        @pl.when(c == 0)
        def _():
            sf_ref[...] = jnp.zeros_like(sf_ref)

        cm = c_ref[...]
        g = lax.dot_general(cm, bm, (((1,), (1,)), ((), ())), preferred_element_type=F32)
        q_row = q.T
        row = lax.broadcasted_iota(jnp.int32, (t, t), 0)
        col = lax.broadcasted_iota(jnp.int32, (t, t), 1)
        lower = row >= col
        upper = row <= col
        xdt_f = (x * dt_x[:, :gw]).astype(BF16)
        xdt_b = (x * dt_x[:, gw:]).astype(BF16)
        lane_t = lax.broadcasted_iota(jnp.int32, (t, LANES), 1)
        pieces = []
        for pair in range(SSD_RANK // 2):
            sl = slice(pair * LANES, (pair + 1) * LANES)
            rhs = jnp.concatenate([xdt_f[:, sl], xdt_b[:, sl]], axis=0)
            ys = []
            for r in (2 * pair, 2 * pair + 1):
                rb = SSD_RANK + r
                lf = jnp.where(lower, jnp.exp(q[:, r:r + 1] - q_row[r:r + 1, :]), 0.0)
                lb = jnp.where(upper, jnp.exp(q_row[rb:rb + 1, :] - q[:, rb:rb + 1]), 0.0)
                m = jnp.concatenate([(g * lf).astype(BF16), (g * lb).astype(BF16)], axis=1)
                ys.append(jnp.dot(m, rhs, preferred_element_type=F32))
            pieces.append(jnp.where(lane_t < SSD_HEAD_DIM, ys[0], ys[1]))
        y = jnp.concatenate(pieces, axis=1)

        sf = sf_ref[...]
        y += jnp.dot(cm, sf.astype(BF16), preferred_element_type=F32) * jnp.exp(q_x[:, :gw])
        y += jnp.dot(cm, sball_ref[c], preferred_element_type=F32) * jnp.exp(tot_x[:, gw:] - q_x[:, gw:])
        w = (x * dt_x[:, :gw] * jnp.exp(tot_x[:, :gw] - q_x[:, :gw])).astype(BF16)
        upd = lax.dot_general(bm, w, (((0,), (0,)), ((), ())), preferred_element_type=F32)
        sf_ref[...] = jnp.exp(tot_x[:, :gw]) * sf + upd

        y += dskip_ref[...] * x
        z = z_ref[...].astype(F32)
        y = y * (z * jax.nn.sigmoid(z))
        o_ref[...] = _rmsnorm(y, nw_ref[...]).astype(BF16)


def _ssd(xbc, zx, dt_all, par, dskip, nw, layer):
    seq = xbc.shape[0]
    nc = seq // SSD_CHUNK
    t = SSD_CHUNK
    b0 = SSD_WIDTH // SSD_STATE
    c0 = b0 + SSD_GROUPS
    expand = np.zeros((LANES, 2 * SSD_GW), np.float32)
    for d in range(2):
        for r in range(SSD_RANK):
            expand[d * SSD_RANK + r, d * SSD_GW + r * SSD_HEAD_DIM: d * SSD_GW + (r + 1) * SSD_HEAD_DIM] = 1.0
    tri = np.tril(np.ones((t, t), np.float32))

    def cidx(p, c):
        return p * c + (1 - p) * (nc - 1 - c)

    return pl.pallas_call(
        _ssd_body,
        grid=(SSD_GROUPS, 2, nc),
        in_specs=[
            pl.BlockSpec((t, SSD_GW), lambda g, p, c: (cidx(p, c), g)),
            pl.BlockSpec((t, SSD_STATE), lambda g, p, c: (cidx(p, c), b0 + g)),
            pl.BlockSpec((t, SSD_STATE), lambda g, p, c: (p * c, c0 + g)),
            pl.BlockSpec((t, SSD_GW), lambda g, p, c: (p * c, g)),
            pl.BlockSpec((t, LANES), lambda g, p, c: (cidx(p, c), g)),
            pl.BlockSpec((None, None, SUBLANES, LANES), lambda g, p, c: (layer, g, 0, 0)),
            pl.BlockSpec((None, None, 1, SSD_GW), lambda g, p, c: (layer, g, 0, 0)),
            pl.BlockSpec((None, None, 1, SSD_GW), lambda g, p, c: (layer, g, 0, 0)),
            pl.BlockSpec((LANES, 2 * SSD_GW), lambda g, p, c: (0, 0)),
            pl.BlockSpec((t, t), lambda g, p, c: (0, 0)),
        ],
        out_specs=pl.BlockSpec((t, SSD_GW), lambda g, p, c: (p * c, g)),
        out_shape=jax.ShapeDtypeStruct((seq, SSD_WIDTH), BF16),
        scratch_shapes=[
            pltpu.VMEM((SSD_STATE, SSD_GW), F32),
            pltpu.VMEM((SSD_STATE, SSD_GW), F32),
            pltpu.VMEM((nc, SSD_STATE, SSD_GW), BF16),
        ],
        compiler_params=_cparams("arbitrary", "arbitrary", "arbitrary"),
        name="ssd_scan",
    )(xbc, xbc, xbc, zx, dt_all, par, dskip, nw, jnp.asarray(expand), jnp.asarray(tri))


def _fw_body(cd_ref, sd_ref, w_ref, a_ref, b_ref):
    w = w_ref[...]
    a_ref[...] = jnp.dot(cd_ref[...], w, preferred_element_type=F32, precision=HIGHEST).astype(BF16)
    b_ref[...] = jnp.dot(sd_ref[...], w, preferred_element_type=F32, precision=HIGHEST).astype(BF16)


def _fourier_weights(fourier_w, layer):
    d = FOURIER_GD
    ang = 2.0 * np.pi * np.outer(np.arange(d), np.arange(d)) / d
    cd = jnp.asarray((np.cos(ang) / np.sqrt(d)).astype(np.float32))
    sd = jnp.asarray((np.sin(ang) / np.sqrt(d)).astype(np.float32))
    return pl.pallas_call(
        _fw_body,
        grid=(FOURIER_GROUPS,),
        in_specs=[
            pl.BlockSpec((d, d), lambda g: (0, 0)),
            pl.BlockSpec((d, d), lambda g: (0, 0)),
            pl.BlockSpec((None, None, d, d), lambda g: (layer, g, 0, 0)),
        ],
        out_specs=[pl.BlockSpec((None, d, d), lambda g: (g, 0, 0))] * 2,
        out_shape=[jax.ShapeDtypeStruct((FOURIER_GROUPS, d, d), BF16)] * 2,
        compiler_params=_cparams("parallel"),
        name="fourier_weights",
    )(cd, sd, fourier_w)


DFT_NB = 4


def _dft_a_body(x_ref, f_ref, tc_ref, ts_ref, o_ref):
    n1 = DFT_N1
    y = jnp.dot(f_ref[...], x_ref[...], preferred_element_type=F32)
    reps = FOURIER_WIDTH // LANES
    for b in range(tc_ref.shape[0]):
        sl = slice(b * FOURIER_WIDTH, (b + 1) * FOURIER_WIDTH)
        yr = y[:n1, sl]
        yi = y[n1:, sl]
        tc = jnp.tile(tc_ref[b], (1, reps))
        ts = jnp.tile(ts_ref[b], (1, reps))
        o_ref[:n1, sl] = (yr * tc + yi * ts).astype(BF16)
        o_ref[n1:, sl] = (yi * tc - yr * ts).astype(BF16)


def _dft_a(u):
    seq = u.shape[0]
    n1 = DFT_N1
    n2 = seq // n1
    nb = min(DFT_NB, n2)
    ang1 = 2.0 * np.pi * np.outer(np.arange(n1), np.arange(n1)) / n1
    f1 = jnp.asarray(np.concatenate([np.cos(ang1), -np.sin(ang1)], axis=0), dtype=BF16)
    angt = 2.0 * np.pi * np.outer(np.arange(n2), np.arange(n1)) / seq
    tc = jnp.asarray(np.repeat(np.cos(angt)[:, :, None], LANES, axis=2).astype(np.float32))
    ts = jnp.asarray(np.repeat(np.sin(angt)[:, :, None], LANES, axis=2).astype(np.float32))
    x2 = u.reshape(n1, n2 * FOURIER_WIDTH)
    return pl.pallas_call(
        _dft_a_body,
        grid=(n2 // nb,),
        in_specs=[
            pl.BlockSpec((n1, nb * FOURIER_WIDTH), lambda j: (0, j)),
            pl.BlockSpec((2 * n1, n1), lambda j: (0, 0)),
            pl.BlockSpec((nb, n1, LANES), lambda j: (j, 0, 0)),
            pl.BlockSpec((nb, n1, LANES), lambda j: (j, 0, 0)),
        ],
        out_specs=pl.BlockSpec((2 * n1, nb * FOURIER_WIDTH), lambda j: (0, j)),
        out_shape=jax.ShapeDtypeStruct((2 * n1, n2 * FOURIER_WIDTH), BF16),
        compiler_params=_cparams("parallel"),
        name="dft_stage_a",
    )(x2, f1, tc, ts)


def _dft_b_body(yr_ref, yi_ref, lr_ref, li_ref, a_ref, b_ref, o_ref):
    kb, n2, width = yr_ref.shape
    rhs = jnp.concatenate([yr_ref[...].reshape(kb * n2, width), yi_ref[...].reshape(kb * n2, width)], axis=0)
    zr = jnp.dot(lr_ref[...], rhs, preferred_element_type=F32).astype(BF16)
    zi = jnp.dot(li_ref[...], rhs, preferred_element_type=F32).astype(BF16)
    outs = []
    for g in range(FOURIER_GROUPS):
        sl = slice(g * FOURIER_GD, (g + 1) * FOURIER_GD)
        outs.append(jnp.dot(zr[:, sl], a_ref[g], preferred_element_type=F32)
                    + jnp.dot(zi[:, sl], b_ref[g], preferred_element_type=F32))
    o_ref[...] = jnp.concatenate(outs, axis=1).reshape(o_ref.shape)


def _dft_b(ya, fa, fb, seq):
    n1 = DFT_N1
    n2 = seq // n1
    kb = DFT_KB
    ang2 = 2.0 * np.pi * np.outer(np.arange(n2), np.arange(n2)) / n2
    c2 = np.cos(ang2) / np.sqrt(seq)
    s2 = np.sin(ang2) / np.sqrt(seq)
    eye = np.eye(kb)
    lr = np.concatenate([np.einsum('ab,kn->kabn', eye, c2).reshape(n2 * kb, kb * n2),
                         np.einsum('ab,kn->kabn', eye, s2).reshape(n2 * kb, kb * n2)], axis=1)
    li = np.concatenate([np.einsum('ab,kn->kabn', eye, -s2).reshape(n2 * kb, kb * n2),
                         np.einsum('ab,kn->kabn', eye, c2).reshape(n2 * kb, kb * n2)], axis=1)
    y3 = ya.reshape(2 * n1, n2, FOURIER_WIDTH)
    nk = n1 // kb
    out = pl.pallas_call(
        _dft_b_body,
        grid=(nk,),
        in_specs=[
            pl.BlockSpec((kb, n2, FOURIER_WIDTH), lambda j: (j, 0, 0)),
            pl.BlockSpec((kb, n2, FOURIER_WIDTH), lambda j: (nk + j, 0, 0)),
            pl.BlockSpec((n2 * kb, 2 * kb * n2), lambda j: (0, 0)),
            pl.BlockSpec((n2 * kb, 2 * kb * n2), lambda j: (0, 0)),
            pl.BlockSpec((FOURIER_GROUPS, FOURIER_GD, FOURIER_GD), lambda j: (0, 0, 0)),
            pl.BlockSpec((FOURIER_GROUPS, FOURIER_GD, FOURIER_GD), lambda j: (0, 0, 0)),
        ],
        out_specs=pl.BlockSpec((n2, kb, FOURIER_WIDTH), lambda j: (0, j, 0)),
        out_shape=jax.ShapeDtypeStruct((n2, n1, FOURIER_WIDTH), F32),
        compiler_params=_cparams("parallel"),
        name="dft_stage_b",
    )(y3, y3, jnp.asarray(lr, dtype=BF16), jnp.asarray(li, dtype=BF16), fa, fb)
    return out.reshape(seq, FOURIER_WIDTH)


def _outproj_body(*refs, n_lhs):
    x_ref = refs[0]
    lhs = refs[1:1 + n_lhs]
    ws = refs[1 + n_lhs:1 + 2 * n_lhs]
    o_ref = refs[1 + 2 * n_lhs]
    acc = x_ref[...]
    for a_ref, w_ref in zip(lhs, ws):
        acc = acc + jnp.dot(a_ref[...].astype(BF16), w_ref[...], preferred_element_type=F32)
    o_ref[...] = acc


def _outproj(x, lhs_list, w, layer):
    seq = x.shape[0]
    in_specs = [pl.BlockSpec((TM, TN), lambda i, j: (i, j))]
    for a in lhs_list:
        in_specs.append(pl.BlockSpec((TM, a.shape[1]), lambda i, j: (i, 0)))
    row = 0
    for a in lhs_list:
        k = a.shape[1]
        assert row % k == 0
        in_specs.append(pl.BlockSpec((None, k, TN), lambda i, j, rb=row // k: (layer, rb, j)))
        row += k
    return pl.pallas_call(
        functools.partial(_outproj_body, n_lhs=len(lhs_list)),
        grid=(seq // TM, D_MODEL // TN),
        in_specs=in_specs,
        out_specs=pl.BlockSpec((TM, TN), lambda i, j: (i, j)),
        out_shape=jax.ShapeDtypeStruct((seq, D_MODEL), F32),
        compiler_params=_cparams("parallel", "parallel"),
        name="outproj",
    )(x, *lhs_list, *([w] * len(lhs_list)))


SGU_TILES = 2 * SGU_WIDTH // TN
SGU_U_TILES = SGU_WIDTH // TN


def _sgu_body(x_ref, nw_ref, w_ref, b_ref, vnw_ref, ws_ref, bs_ref, o_ref, h_ref, uv_ref, ss_ref):
    j = pl.program_id(1)
    tm = x_ref.shape[0]

    @pl.when(j == 0)
    def _():
        h_ref[...] = _rmsnorm(x_ref[...], nw_ref[...]).astype(BF16)
        ss_ref[...] = jnp.zeros_like(ss_ref)

    acc = jnp.dot(h_ref[...], w_ref[...], preferred_element_type=F32) + b_ref[...]
    act = jax.nn.gelu(acc)
    uv_ref[j] = act.astype(BF16)

    @pl.when(j >= SGU_U_TILES)
    def _():
        ss_ref[...] += jnp.sum(act * act, axis=-1, keepdims=True)

    @pl.when(j == SGU_TILES - 1)
    def _():
        rs = lax.rsqrt(ss_ref[...] * (1.0 / SGU_WIDTH) + EPS)
        per_tile = TN // SGU_GD
        for g in range(SGU_GROUPS):
            tile, off = divmod(g, per_tile)
            sl = slice(off * SGU_GD, (off + 1) * SGU_GD)
            v = uv_ref[SGU_U_TILES + tile, :, sl].astype(F32)
            v = (v * rs * vnw_ref[:, g * SGU_GD:(g + 1) * SGU_GD]).astype(BF16)
            bias = jnp.tile(bs_ref[g], (1, SGU_GD // LANES))
            for qc in range(tm // SGU_CHUNK):
                rows = slice(qc * SGU_CHUNK, (qc + 1) * SGU_CHUNK)
                mixed = jnp.dot(ws_ref[g], v[rows], preferred_element_type=F32) + bias
                u = uv_ref[tile, rows, sl].astype(F32)
                o_ref[rows, g * SGU_GD:(g + 1) * SGU_GD] = (u * mixed).astype(BF16)


def _sgu(x, nw, w_uv, b_uv, vnw, w_s, b_s, layer, j_odd):
    seq = x.shape[0]
    return pl.pallas_call(
        _sgu_body,
        grid=(seq // TM, SGU_TILES),
        in_specs=[
            pl.BlockSpec((TM, D_MODEL), lambda i, j: (i, 0)),
            pl.BlockSpec((None, 1, D_MODEL), lambda i, j: (layer, 0, 0)),
            pl.BlockSpec((None, D_MODEL, TN), lambda i, j: (j_odd, 0, j)),
            pl.BlockSpec((None, 1, TN), lambda i, j: (j_odd, 0, j)),
            pl.BlockSpec((None, 1, SGU_WIDTH), lambda i, j: (j_odd, 0, 0)),
            pl.BlockSpec((None, SGU_GROUPS, SGU_CHUNK, SGU_CHUNK), lambda i, j: (j_odd, 0, 0, 0)),
            pl.BlockSpec((None, SGU_GROUPS, SGU_CHUNK, LANES), lambda i, j: (j_odd, 0, 0, 0)),
        ],
        out_specs=pl.BlockSpec((TM, SGU_WIDTH), lambda i, j: (i, 0)),
        out_shape=jax.ShapeDtypeStruct((seq, SGU_WIDTH), BF16),
        scratch_shapes=[
            pltpu.VMEM((TM, D_MODEL), BF16),
            pltpu.VMEM((SGU_TILES, TM, TN), BF16),
            pltpu.VMEM((TM, 1), F32),
        ],
        compiler_params=_cparams("parallel", "arbitrary"),
        name="sgu",
    )(x, nw, w_uv, b_uv, vnw, w_s, b_s)


def _norm_body(x_ref, w_ref, o_ref):
    o_ref[...] = _rmsnorm(x_ref[...], w_ref[...])


def _final_norm(x, w):
    seq = x.shape[0]
    return pl.pallas_call(
        _norm_body,
        grid=(seq // TM,),
        in_specs=[pl.BlockSpec((TM, D_MODEL), lambda i: (i, 0)), pl.BlockSpec((1, D_MODEL), lambda i: (0, 0))],
        out_specs=pl.BlockSpec((TM, D_MODEL), lambda i: (i, 0)),
        out_shape=jax.ShapeDtypeStruct((seq, D_MODEL), F32),
        compiler_params=_cparams("parallel"),
        name="final_norm",
    )(x, w)


def _even_in_weight(w):
    d = w.shape[0]
    dt0 = ZX_WIDTH
    u0 = dt0 + 2 * SSD_HEADS
    w_dt = w[:, dt0:u0].reshape(d, 2, SSD_GROUPS, SSD_RANK)
    w_dt = jnp.transpose(w_dt, (0, 2, 1, 3)).reshape(d, SSD_GROUPS, 2 * SSD_RANK)
    w_dt = jnp.pad(w_dt, ((0, 0), (0, 0), (0, LANES - 2 * SSD_RANK))).reshape(d, SSD_GROUPS * LANES)
    return jnp.concatenate([w[:, :dt0], w[:, u0:], w_dt], axis=1).astype(BF16)


def _group_lanes(p):
    n = p.shape[0]
    p = jnp.transpose(p.reshape(n, 2, SSD_GROUPS, SSD_RANK), (0, 2, 1, 3)).reshape(n, SSD_GROUPS, 2 * SSD_RANK)
    return jnp.pad(p, ((0, 0), (0, 0), (0, LANES - 2 * SSD_RANK)))


def kernel(x, ffn1_norm, ffn1_w_gate, ffn1_w_up, ffn1_w_down, mix_norm, ffn2_norm, ffn2_w_gate, ffn2_w_up,
           ffn2_w_down, even_w_in, ssd_conv_w, ssd_conv_b, ssd_dt_bias, ssd_a_log, ssd_d, ssd_norm, fourier_w,
           even_w_out, sgu_w_uv, sgu_b_uv, sgu_norm, sgu_w_s, sgu_b_s, odd_w_out, final_norm):
    bsz, seq, d = x.shape
    assert bsz == 1 and d == D_MODEL
    depth = ffn1_norm.shape[0]
    n_even = even_w_in.shape[0]
    xs = x.reshape(seq, d)

    row3 = lambda a: a.reshape(a.shape[0], 1, a.shape[1])
    f1 = (row3(ffn1_norm), ffn1_w_gate.astype(BF16), ffn1_w_up.astype(BF16), ffn1_w_down.astype(BF16))
    f2 = (row3(ffn2_norm), ffn2_w_gate.astype(BF16), ffn2_w_up.astype(BF16), ffn2_w_down.astype(BF16))
    mixn = row3(mix_norm)
    w_in = [_even_in_weight(even_w_in[j]) for j in range(n_even)]
    zeros = jnp.zeros((n_even, SSD_GROUPS, SUBLANES - 2, LANES), F32)
    par = jnp.concatenate([_group_lanes(ssd_dt_bias)[:, :, None, :], _group_lanes(ssd_a_log)[:, :, None, :], zeros],
                          axis=2)
    dskip = jnp.repeat(ssd_d, SSD_HEAD_DIM, axis=1).reshape(n_even, SSD_GROUPS, 1, SSD_GW)
    ssd_nw = ssd_norm.reshape(n_even, SSD_GROUPS, 1, SSD_GW)
    conv_b = row3(ssd_conv_b)
    w_out_even = even_w_out.astype(BF16)
    w_uv = sgu_w_uv.astype(BF16)
    b_uv = row3(sgu_b_uv)
    sgu_nw = row3(sgu_norm)
    w_s = sgu_w_s.astype(BF16)
    b_s = jnp.broadcast_to(sgu_b_s[..., None], sgu_b_s.shape + (LANES,))
    w_out_odd = odd_w_out.astype(BF16)

    for i in range(depth):
        xs = _ffn(xs, *f1, i)
        j = i // 2
        if i % 2 == 0:
            zx, u, dt_all = _inproj(xs, mixn, w_in[j], i)
            xbc = _conv(zx, ssd_conv_w, conv_b, j)
            y_ssd = _ssd(xbc, zx, dt_all, par, dskip, ssd_nw, j)
            fa, fb = _fourier_weights(fourier_w, j)
            y_fft = _dft_b(_dft_a(u), fa, fb, seq)
            xs = _outproj(xs, [y_ssd, y_fft], w_out_even, j)
        else:
            gated = _sgu(xs, mixn, w_uv, b_uv, sgu_nw, w_s, b_s, i, j)
            xs = _outproj(xs, [gated], w_out_odd, j)
        xs = _ffn(xs, *f2, i)
    return _final_norm(xs, final_norm.reshape(1, d)).reshape(bsz, seq, d)
```

```python
import functools

import numpy as np
import jax
import jax.numpy as jnp
from jax import lax
from jax.experimental import pallas as pl
from jax.experimental.pallas import tpu as pltpu

F32 = jnp.float32
BF16 = jnp.bfloat16
HIGHEST = lax.Precision.HIGHEST

D_MODEL = 2048
D_FF = 5632
EPS = 1e-6
SSD_HEAD_DIM = 64
SSD_HEADS = 48
SSD_GROUPS = 8
SSD_RANK = SSD_HEADS // SSD_GROUPS
SSD_GW = SSD_RANK * SSD_HEAD_DIM
SSD_STATE = 128
SSD_CHUNK = 256
SSD_WIDTH = SSD_HEADS * SSD_HEAD_DIM
SSD_CONV = 5
SSD_CONV_CH = SSD_WIDTH + 2 * SSD_GROUPS * SSD_STATE
FOURIER_WIDTH = 1024
FOURIER_GROUPS = 4
FOURIER_GD = FOURIER_WIDTH // FOURIER_GROUPS
SGU_WIDTH = 4096
SGU_GROUPS = 8
SGU_GD = SGU_WIDTH // SGU_GROUPS
SGU_CHUNK = 128
ZX_WIDTH = SSD_WIDTH + SSD_CONV_CH

LANES = 128
SUBLANES = 8
VMEM_LIMIT = 56 * 1024 * 1024

TM = 512
TMX = 1024
TF = 512
TN = 1024
DFT_N1 = 128
DFT_KB = 8


def _cparams(*sem):
    return pltpu.CompilerParams(dimension_semantics=sem, vmem_limit_bytes=VMEM_LIMIT)


def _rmsnorm(x, w):
    ms = jnp.mean(x * x, axis=-1, keepdims=True)
    return x * lax.rsqrt(ms + EPS) * w


def _ffn_body(x_ref, nw_ref, wg_ref, wu_ref, wd_ref, o_ref, xn_ref):
    j = pl.program_id(1)

    @pl.when(j == 0)
    def _():
        x = x_ref[...]
        xn_ref[...] = _rmsnorm(x, nw_ref[...]).astype(BF16)
        o_ref[...] = x

    xn = xn_ref[...]
    g = jnp.dot(xn, wg_ref[...], preferred_element_type=F32)
    u = jnp.dot(xn, wu_ref[...], preferred_element_type=F32)
    h = ((0.5 * g) * jax.nn.sigmoid(g) * u).astype(BF16)
    o_ref[...] += jnp.dot(h, wd_ref[...], preferred_element_type=F32)


def _ffn(x, nw, wg, wu, wd, layer):
    seq = x.shape[0]
    tm = min(TMX, seq)
    return pl.pallas_call(
        _ffn_body,
        grid=(seq // tm, D_FF // TF),
        in_specs=[
            pl.BlockSpec((tm, D_MODEL), lambda i, j: (i, 0), pipeline_mode=pl.Buffered(1)),
            pl.BlockSpec((None, 1, D_MODEL), lambda i, j: (layer, 0, 0)),
            pl.BlockSpec((None, D_MODEL, TF), lambda i, j: (layer, 0, j)),
            pl.BlockSpec((None, D_MODEL, TF), lambda i, j: (layer, 0, j)),
            pl.BlockSpec((None, TF, D_MODEL), lambda i, j: (layer, j, 0)),
        ],
        out_specs=pl.BlockSpec((tm, D_MODEL), lambda i, j: (i, 0)),
        out_shape=jax.ShapeDtypeStruct((seq, D_MODEL), F32),
        scratch_shapes=[pltpu.VMEM((tm, D_MODEL), BF16)],
        compiler_params=_cparams("parallel", "arbitrary"),
        name="ffn",
    )(x, nw, wg, wu, wd)


N_ZX_TILES = ZX_WIDTH // TN
IN_TILES = N_ZX_TILES + 2


def _inproj_body(x_ref, nw_ref, w_ref, zx_ref, u_ref, dt_ref, h_ref):
    j = pl.program_id(1)

    @pl.when(j == 0)
    def _():
        h_ref[...] = _rmsnorm(x_ref[...], nw_ref[...]).astype(BF16)

    acc = jnp.dot(h_ref[...], w_ref[...], preferred_element_type=F32)

    @pl.when(j < N_ZX_TILES)
    def _():
        zx_ref[...] = acc.astype(BF16)

    @pl.when(j == N_ZX_TILES)
    def _():
        u_ref[...] = acc.astype(BF16)

    @pl.when(j == N_ZX_TILES + 1)
    def _():
        dt_ref[...] = acc


def _inproj(x, nw, w, layer):
    seq = x.shape[0]
    tm = min(TMX, seq)
    return pl.pallas_call(
        _inproj_body,
        grid=(seq // tm, IN_TILES),
        in_specs=[
            pl.BlockSpec((tm, D_MODEL), lambda i, j: (i, 0), pipeline_mode=pl.Buffered(1)),
            pl.BlockSpec((None, 1, D_MODEL), lambda i, j: (layer, 0, 0)),
            pl.BlockSpec((D_MODEL, TN), lambda i, j: (0, j)),
        ],
        out_specs=[
            pl.BlockSpec((tm, TN), lambda i, j: (i, jnp.minimum(j, N_ZX_TILES - 1))),
            pl.BlockSpec((tm, FOURIER_WIDTH), lambda i, j: (i, 0)),
            pl.BlockSpec((tm, SSD_GROUPS * LANES), lambda i, j: (i, 0)),
        ],
        out_shape=[
            jax.ShapeDtypeStruct((seq, ZX_WIDTH), BF16),
            jax.ShapeDtypeStruct((seq, FOURIER_WIDTH), BF16),
            jax.ShapeDtypeStruct((seq, SSD_GROUPS * LANES), F32),
        ],
        scratch_shapes=[pltpu.VMEM((tm, D_MODEL), BF16)],
        compiler_params=_cparams("parallel", "arbitrary"),
        name="even_inproj",
    )(x, nw, w)


CONV_TR = 1024
CONV_TC = 512
CONV_HALO = 16


def _conv_body(xm_ref, xp_ref, xn_ref, w_ref, b_ref, o_ref, ext_ref):
    i = pl.program_id(0)
    last = pl.num_programs(0) - 1
    tr = xm_ref.shape[0]
    prev = xp_ref[...].astype(F32)[CONV_HALO - SUBLANES:]
    nxt = xn_ref[...].astype(F32)[:SUBLANES]
    ext_ref[0:SUBLANES, :] = jnp.where(i == 0, 0.0, prev)
    ext_ref[SUBLANES:SUBLANES + tr, :] = xm_ref[...].astype(F32)
    ext_ref[SUBLANES + tr:, :] = jnp.where(i == last, 0.0, nxt)
    w = w_ref[...]
    acc = jnp.broadcast_to(b_ref[...], o_ref.shape)
    half = SSD_CONV // 2
    for k in range(SSD_CONV):
        acc = acc + ext_ref[pl.ds(SUBLANES + k - half, tr), :] * w[k:k + 1, :]
    o_ref[...] = (acc * jax.nn.sigmoid(acc)).astype(BF16)


def _conv(zx, conv_w, conv_b, layer):
    seq = zx.shape[0]
    tr = min(CONV_TR, seq)
    col0 = SSD_WIDTH // CONV_TC
    hb = tr // CONV_HALO
    nhb = seq // CONV_HALO
    return pl.pallas_call(
        _conv_body,
        grid=(seq // tr, SSD_CONV_CH // CONV_TC),
        in_specs=[
            pl.BlockSpec((tr, CONV_TC), lambda i, j: (i, col0 + j)),
            pl.BlockSpec((CONV_HALO, CONV_TC), lambda i, j: (jnp.maximum(i * hb - 1, 0), col0 + j)),
            pl.BlockSpec((CONV_HALO, CONV_TC), lambda i, j: (jnp.minimum((i + 1) * hb, nhb - 1), col0 + j)),
            pl.BlockSpec((None, SSD_CONV, CONV_TC), lambda i, j: (layer, 0, j)),
            pl.BlockSpec((None, 1, CONV_TC), lambda i, j: (layer, 0, j)),
        ],
        out_specs=pl.BlockSpec((tr, CONV_TC), lambda i, j: (i, j)),
        out_shape=jax.ShapeDtypeStruct((seq, SSD_CONV_CH), BF16),
        scratch_shapes=[pltpu.VMEM((tr + 2 * SUBLANES, CONV_TC), F32)],
        compiler_params=_cparams("parallel", "parallel"),
        name="ssd_conv",
    )(zx, zx, zx, conv_w, conv_b)


def _softplus(v):
    return jnp.maximum(v, 0.0) + jnp.log1p(jnp.exp(-jnp.abs(v)))


def _head_rows(rows, first):
    n = rows.shape[1]
    return jnp.concatenate([jnp.broadcast_to(rows[first + h:first + h + 1, :], (SSD_HEAD_DIM, n))
                            for h in range(SSD_RANK)], axis=0)


def _ssd_body(x_ref, b_ref, c_ref, z_ref, dt_ref, par_ref, dskip_ref, nw_ref, tri_ref,
              o_ref, sf_ref, sb_ref, sball_ref):
    phase = pl.program_id(1)
    c = pl.program_id(2)
    nc = pl.num_programs(2)
    t = SSD_CHUNK
    rk = SSD_RANK
    hd = SSD_HEAD_DIM
    prm = 2 * SUBLANES

    par = par_ref[...]
    lane = lax.broadcasted_iota(jnp.int32, (1, LANES), 1)
    dt = _softplus(dt_ref[...] + par[0:1, :])
    da = dt * (-jnp.exp(par[1:2, :]))
    d1 = da.astype(BF16)
    r1 = da - d1.astype(F32)
    d2 = r1.astype(BF16)
    d3 = (r1 - d2.astype(F32)).astype(BF16)
    cs3 = jnp.dot(tri_ref[...], jnp.concatenate([d1, d2, d3], axis=1), preferred_element_type=F32)
    cs = cs3[:, :LANES] + cs3[:, LANES:2 * LANES] + cs3[:, 2 * LANES:]
    q = cs - jnp.where(lane >= rk, da, 0.0)
    dt_t = dt.T[:prm]
    q_t = q.T[:prm]
    tot_t = jnp.broadcast_to(cs.T[:prm, t - 1:t], (prm, t))
    x_t = x_ref[...].astype(F32).T
    bm = b_ref[...]

    @pl.when(phase == 0)
    def _():
        @pl.when(c == 0)
        def _():
            sb_ref[...] = jnp.zeros_like(sb_ref)

        cc = nc - 1 - c
        sb = sb_ref[...]
        sball_ref[cc] = sb.astype(BF16)
        wb = dt_t * jnp.exp(q_t)
        xw = (x_t * _head_rows(wb, rk)).astype(BF16)
        upd = jnp.dot(xw, bm, preferred_element_type=F32)
        sb_ref[...] = _head_rows(jnp.exp(tot_t[:, :SSD_STATE]), rk) * sb + upd

    @pl.when(phase == 1)
    def _():
        @pl.when(c == 0)
        def _():
            sf_ref[...] = jnp.zeros_like(sf_ref)

        cm = c_ref[...]
        gt = lax.dot_general(bm, cm, (((1,), (1,)), ((), ())), preferred_element_type=F32)
        srow = lax.broadcasted_iota(jnp.int32, (t, t), 0)
        tcol = lax.broadcasted_iota(jnp.int32, (t, t), 1)
        causal = srow <= tcol
        anti = srow >= tcol
        ys = []
        for h in range(rk):
            hb = rk + h
            xh = x_t[h * hd:(h + 1) * hd, :]
            lhs = jnp.concatenate([(xh * dt_t[h:h + 1, :]).astype(BF16), (xh * dt_t[hb:hb + 1, :]).astype(BF16)],
                                  axis=1)
            lf = jnp.where(causal, jnp.exp(q_t[h:h + 1, :] - q[:, h:h + 1]), 0.0)
            lb = jnp.where(anti, jnp.exp(q[:, hb:hb + 1] - q_t[hb:hb + 1, :]), 0.0)
            rhs = jnp.concatenate([(gt * lf).astype(BF16), (gt * lb).astype(BF16)], axis=0)
            ys.append(jnp.dot(lhs, rhs, preferred_element_type=F32))
        y_t = jnp.concatenate(ys, axis=0)

        sf = sf_ref[...]
        states = jnp.concatenate([sf.astype(BF16), sball_ref[c]], axis=0)
        off = lax.dot_general(states, cm, (((1,), (1,)), ((), ())), preferred_element_type=F32)
        y_t += off[:SSD_GW] * _head_rows(jnp.exp(q_t), 0)
        y_t += off[SSD_GW:] * _head_rows(jnp.exp(tot_t - q_t), rk)
        wf = dt_t * jnp.exp(tot_t - q_t)
        xw = (x_t * _head_rows(wf, 0)).astype(BF16)
        upd = jnp.dot(xw, bm, preferred_element_type=F32)
        sf_ref[...] = _head_rows(jnp.exp(tot_t[:, :SSD_STATE]), 0) * sf + upd

        y = y_t.T + dskip_ref[...] * x_ref[...].astype(F32)
        z = z_ref[...].astype(F32)
        y = y * (z * jax.nn.sigmoid(z))
        o_ref[...] = _rmsnorm(y, nw_ref[...]).astype(BF16)


def _ssd(xbc, zx, dt_all, par, dskip, nw, layer):
    seq = xbc.shape[0]
    nc = seq // SSD_CHUNK
    t = SSD_CHUNK
    b0 = SSD_WIDTH // SSD_STATE
    c0 = b0 + SSD_GROUPS
    tri = jnp.asarray(np.tril(np.ones((t, t), np.float32)), dtype=BF16)

    def cidx(p, c):
        return p * c + (1 - p) * (nc - 1 - c)

    return pl.pallas_call(
        _ssd_body,
        grid=(SSD_GROUPS, 2, nc),
        in_specs=[
            pl.BlockSpec((t, SSD_GW), lambda g, p, c: (cidx(p, c), g)),
            pl.BlockSpec((t, SSD_STATE), lambda g, p, c: (cidx(p, c), b0 + g)),
            pl.BlockSpec((t, SSD_STATE), lambda g, p, c: (p * c, c0 + g)),
            pl.BlockSpec((t, SSD_GW), lambda g, p, c: (p * c, g)),
            pl.BlockSpec((t, LANES), lambda g, p, c: (cidx(p, c), g)),
            pl.BlockSpec((None, None, SUBLANES, LANES), lambda g, p, c: (layer, g, 0, 0)),
            pl.BlockSpec((None, None, 1, SSD_GW), lambda g, p, c: (layer, g, 0, 0)),
            pl.BlockSpec((None, None, 1, SSD_GW), lambda g, p, c: (layer, g, 0, 0)),
            pl.BlockSpec((t, t), lambda g, p, c: (0, 0)),
        ],
        out_specs=pl.BlockSpec((t, SSD_GW), lambda g, p, c: (p * c, g)),
        out_shape=jax.ShapeDtypeStruct((seq, SSD_WIDTH), BF16),
        scratch_shapes=[
            pltpu.VMEM((SSD_GW, SSD_STATE), F32),
            pltpu.VMEM((SSD_GW, SSD_STATE), F32),
            pltpu.VMEM((nc, SSD_GW, SSD_STATE), BF16),
        ],
        compiler_params=_cparams("arbitrary", "arbitrary", "arbitrary"),
        name="ssd_scan",
    )(xbc, xbc, xbc, zx, dt_all, par, dskip, nw, tri)


def _fw_body(cd_ref, sd_ref, w_ref, a_ref, b_ref):
    w = w_ref[...]
    a_ref[...] = jnp.dot(cd_ref[...], w, preferred_element_type=F32, precision=HIGHEST).astype(BF16)
    b_ref[...] = jnp.dot(sd_ref[...], w, preferred_element_type=F32, precision=HIGHEST).astype(BF16)


def _fourier_weights(fourier_w, layer):
    d = FOURIER_GD
    ang = 2.0 * np.pi * np.outer(np.arange(d), np.arange(d)) / d
    cd = jnp.asarray((np.cos(ang) / np.sqrt(d)).astype(np.float32))
    sd = jnp.asarray((np.sin(ang) / np.sqrt(d)).astype(np.float32))
    return pl.pallas_call(
        _fw_body,
        grid=(FOURIER_GROUPS,),
        in_specs=[
            pl.BlockSpec((d, d), lambda g: (0, 0)),
            pl.BlockSpec((d, d), lambda g: (0, 0)),
            pl.BlockSpec((None, None, d, d), lambda g: (layer, g, 0, 0)),
        ],
        out_specs=[pl.BlockSpec((None, d, d), lambda g: (g, 0, 0))] * 2,
        out_shape=[jax.ShapeDtypeStruct((FOURIER_GROUPS, d, d), BF16)] * 2,
        compiler_params=_cparams("parallel"),
        name="fourier_weights",
    )(cd, sd, fourier_w)


DFT_NB = 4


def _dft_a_body(x_ref, f_ref, tc_ref, ts_ref, o_ref):
    n1 = DFT_N1
    y = jnp.dot(f_ref[...], x_ref[...], preferred_element_type=F32)
    reps = FOURIER_WIDTH // LANES
    for b in range(tc_ref.shape[0]):
        sl = slice(b * FOURIER_WIDTH, (b + 1) * FOURIER_WIDTH)
        yr = y[:n1, sl]
        yi = y[n1:, sl]
        tc = jnp.tile(tc_ref[b], (1, reps))
        ts = jnp.tile(ts_ref[b], (1, reps))
        o_ref[:n1, sl] = (yr * tc + yi * ts).astype(BF16)
        o_ref[n1:, sl] = (yi * tc - yr * ts).astype(BF16)


def _dft_a(u):
    seq = u.shape[0]
    n1 = DFT_N1
    n2 = seq // n1
    nb = min(DFT_NB, n2)
    ang1 = 2.0 * np.pi * np.outer(np.arange(n1), np.arange(n1)) / n1
    f1 = jnp.asarray(np.concatenate([np.cos(ang1), -np.sin(ang1)], axis=0), dtype=BF16)
    angt = 2.0 * np.pi * np.outer(np.arange(n2), np.arange(n1)) / seq
    tc = jnp.asarray(np.repeat(np.cos(angt)[:, :, None], LANES, axis=2).astype(np.float32))
    ts = jnp.asarray(np.repeat(np.sin(angt)[:, :, None], LANES, axis=2).astype(np.float32))
    x2 = u.reshape(n1, n2 * FOURIER_WIDTH)
    return pl.pallas_call(
        _dft_a_body,
        grid=(n2 // nb,),
        in_specs=[
            pl.BlockSpec((n1, nb * FOURIER_WIDTH), lambda j: (0, j)),
            pl.BlockSpec((2 * n1, n1), lambda j: (0, 0)),
            pl.BlockSpec((nb, n1, LANES), lambda j: (j, 0, 0)),
            pl.BlockSpec((nb, n1, LANES), lambda j: (j, 0, 0)),
        ],
        out_specs=pl.BlockSpec((2 * n1, nb * FOURIER_WIDTH), lambda j: (0, j)),
        out_shape=jax.ShapeDtypeStruct((2 * n1, n2 * FOURIER_WIDTH), BF16),
        compiler_params=_cparams("parallel"),
        name="dft_stage_a",
    )(x2, f1, tc, ts)


def _dft_b_body(yr_ref, yi_ref, lr_ref, li_ref, a_ref, b_ref, o_ref):
    kb, n2, width = yr_ref.shape
    rhs = jnp.concatenate([yr_ref[...].reshape(kb * n2, width), yi_ref[...].reshape(kb * n2, width)], axis=0)
    zr = jnp.dot(lr_ref[...], rhs, preferred_element_type=F32).astype(BF16)
    zi = jnp.dot(li_ref[...], rhs, preferred_element_type=F32).astype(BF16)
    outs = []
    for g in range(FOURIER_GROUPS):
        sl = slice(g * FOURIER_GD, (g + 1) * FOURIER_GD)
        outs.append(jnp.dot(zr[:, sl], a_ref[g], preferred_element_type=F32)
                    + jnp.dot(zi[:, sl], b_ref[g], preferred_element_type=F32))
    o_ref[...] = jnp.concatenate(outs, axis=1).reshape(o_ref.shape)


def _dft_b(ya, fa, fb, seq):
    n1 = DFT_N1
    n2 = seq // n1
    kb = DFT_KB
    ang2 = 2.0 * np.pi * np.outer(np.arange(n2), np.arange(n2)) / n2
    c2 = np.cos(ang2) / np.sqrt(seq)
    s2 = np.sin(ang2) / np.sqrt(seq)
    eye = np.eye(kb)
    lr = np.concatenate([np.einsum('ab,kn->kabn', eye, c2).reshape(n2 * kb, kb * n2),
                         np.einsum('ab,kn->kabn', eye, s2).reshape(n2 * kb, kb * n2)], axis=1)
    li = np.concatenate([np.einsum('ab,kn->kabn', eye, -s2).reshape(n2 * kb, kb * n2),
                         np.einsum('ab,kn->kabn', eye, c2).reshape(n2 * kb, kb * n2)], axis=1)
    y3 = ya.reshape(2 * n1, n2, FOURIER_WIDTH)
    nk = n1 // kb
    out = pl.pallas_call(
        _dft_b_body,
        grid=(nk,),
        in_specs=[
            pl.BlockSpec((kb, n2, FOURIER_WIDTH), lambda j: (j, 0, 0)),
            pl.BlockSpec((kb, n2, FOURIER_WIDTH), lambda j: (nk + j, 0, 0)),
            pl.BlockSpec((n2 * kb, 2 * kb * n2), lambda j: (0, 0)),
            pl.BlockSpec((n2 * kb, 2 * kb * n2), lambda j: (0, 0)),
            pl.BlockSpec((FOURIER_GROUPS, FOURIER_GD, FOURIER_GD), lambda j: (0, 0, 0)),
            pl.BlockSpec((FOURIER_GROUPS, FOURIER_GD, FOURIER_GD), lambda j: (0, 0, 0)),
        ],
        out_specs=pl.BlockSpec((n2, kb, FOURIER_WIDTH), lambda j: (0, j, 0)),
        out_shape=jax.ShapeDtypeStruct((n2, n1, FOURIER_WIDTH), F32),
        compiler_params=_cparams("parallel"),
        name="dft_stage_b",
    )(y3, y3, jnp.asarray(lr, dtype=BF16), jnp.asarray(li, dtype=BF16), fa, fb)
    return out.reshape(seq, FOURIER_WIDTH)


def _outproj_body(*refs, n_lhs):
    x_ref = refs[0]
    lhs = refs[1:1 + n_lhs]
    ws = refs[1 + n_lhs:1 + 2 * n_lhs]
    o_ref = refs[1 + 2 * n_lhs]
    acc = x_ref[...]
    for a_ref, w_ref in zip(lhs, ws):
        acc = acc + jnp.dot(a_ref[...].astype(BF16), w_ref[...], preferred_element_type=F32)
    o_ref[...] = acc


def _outproj(x, lhs_list, w, layer):
    seq = x.shape[0]
    in_specs = [pl.BlockSpec((TM, TN), lambda j, i: (i, j))]
    for a in lhs_list:
        in_specs.append(pl.BlockSpec((TM, a.shape[1]), lambda j, i: (i, 0)))
    row = 0
    for a in lhs_list:
        k = a.shape[1]
        assert row % k == 0
        in_specs.append(pl.BlockSpec((None, k, TN), lambda j, i, rb=row // k: (layer, rb, j)))
        row += k
    return pl.pallas_call(
        functools.partial(_outproj_body, n_lhs=len(lhs_list)),
        grid=(D_MODEL // TN, seq // TM),
        in_specs=in_specs,
        out_specs=pl.BlockSpec((TM, TN), lambda j, i: (i, j)),
        out_shape=jax.ShapeDtypeStruct((seq, D_MODEL), F32),
        compiler_params=_cparams("parallel", "parallel"),
        name="outproj",
    )(x, *lhs_list, *([w] * len(lhs_list)))


SGU_V_TILES = SGU_WIDTH // TN
SGU_TILES = 2 * SGU_V_TILES


def _sgu_body(x_ref, nw_ref, w_ref, b_ref, vnw_ref, ws_ref, bs_ref, o_ref, h_ref, v_ref, ss_ref):
    j = pl.program_id(1)
    tm = x_ref.shape[0]

    @pl.when(j == 0)
    def _():
        h_ref[...] = _rmsnorm(x_ref[...], nw_ref[...]).astype(BF16)
        ss_ref[...] = jnp.zeros_like(ss_ref)

    acc = jnp.dot(h_ref[...], w_ref[...], preferred_element_type=F32) + b_ref[...]
    act = jax.nn.gelu(acc)

    @pl.when(j < SGU_V_TILES)
    def _():
        v_ref[j] = act.astype(BF16)
        ss_ref[...] += jnp.sum(act * act, axis=-1, keepdims=True)

    @pl.when(j == SGU_V_TILES - 1)
    def _():
        rs = lax.rsqrt(ss_ref[...] * (1.0 / SGU_WIDTH) + EPS)
        per_tile = TN // SGU_GD
        for g in range(SGU_GROUPS):
            tile, off = divmod(g, per_tile)
            sl = slice(off * SGU_GD, (off + 1) * SGU_GD)
            v = v_ref[tile, :, sl].astype(F32)
            v = (v * rs * vnw_ref[:, g * SGU_GD:(g + 1) * SGU_GD]).astype(BF16)
            bias = jnp.tile(bs_ref[g], (1, SGU_GD // LANES))
            for qc in range(tm // SGU_CHUNK):
                rows = slice(qc * SGU_CHUNK, (qc + 1) * SGU_CHUNK)
                mixed = jnp.dot(ws_ref[g], v[rows], preferred_element_type=F32) + bias
                v_ref[tile, rows, sl] = mixed.astype(BF16)

    @pl.when(j >= SGU_V_TILES)
    def _():
        o_ref[...] = (act * v_ref[j - SGU_V_TILES].astype(F32)).astype(BF16)


def _sgu(x, nw, w_uv, b_uv, vnw, w_s, b_s, layer, j_odd):
    seq = x.shape[0]
    tm = min(TMX, seq)
    wcol = lambda j: (j + SGU_V_TILES) % SGU_TILES
    return pl.pallas_call(
        _sgu_body,
        grid=(seq // tm, SGU_TILES),
        in_specs=[
            pl.BlockSpec((tm, D_MODEL), lambda i, j: (i, 0), pipeline_mode=pl.Buffered(1)),
            pl.BlockSpec((None, 1, D_MODEL), lambda i, j: (layer, 0, 0)),
            pl.BlockSpec((None, D_MODEL, TN), lambda i, j: (j_odd, 0, wcol(j))),
            pl.BlockSpec((None, 1, TN), lambda i, j: (j_odd, 0, wcol(j))),
            pl.BlockSpec((None, 1, SGU_WIDTH), lambda i, j: (j_odd, 0, 0)),
            pl.BlockSpec((None, SGU_GROUPS, SGU_CHUNK, SGU_CHUNK), lambda i, j: (j_odd, 0, 0, 0)),
            pl.BlockSpec((None, SGU_GROUPS, SGU_CHUNK, LANES), lambda i, j: (j_odd, 0, 0, 0)),
        ],
        out_specs=pl.BlockSpec((tm, TN), lambda i, j: (i, jnp.maximum(j - SGU_V_TILES, 0))),
        out_shape=jax.ShapeDtypeStruct((seq, SGU_WIDTH), BF16),
        scratch_shapes=[
            pltpu.VMEM((tm, D_MODEL), BF16),
            pltpu.VMEM((SGU_V_TILES, tm, TN), BF16),
            pltpu.VMEM((tm, 1), F32),
        ],
        compiler_params=_cparams("parallel", "arbitrary"),
        name="sgu",
    )(x, nw, w_uv, b_uv, vnw, w_s, b_s)


def _norm_body(x_ref, w_ref, o_ref):
    o_ref[...] = _rmsnorm(x_ref[...], w_ref[...])


def _final_norm(x, w):
    seq = x.shape[0]
    return pl.pallas_call(
        _norm_body,
        grid=(seq // TM,),
        in_specs=[pl.BlockSpec((TM, D_MODEL), lambda i: (i, 0)), pl.BlockSpec((1, D_MODEL), lambda i: (0, 0))],
        out_specs=pl.BlockSpec((TM, D_MODEL), lambda i: (i, 0)),
        out_shape=jax.ShapeDtypeStruct((seq, D_MODEL), F32),
        compiler_params=_cparams("parallel"),
        name="final_norm",
    )(x, w)


def _even_in_weight(w):
    d = w.shape[0]
    dt0 = ZX_WIDTH
    u0 = dt0 + 2 * SSD_HEADS
    w_dt = w[:, dt0:u0].reshape(d, 2, SSD_GROUPS, SSD_RANK)
    w_dt = jnp.transpose(w_dt, (0, 2, 1, 3)).reshape(d, SSD_GROUPS, 2 * SSD_RANK)
    w_dt = jnp.pad(w_dt, ((0, 0), (0, 0), (0, LANES - 2 * SSD_RANK))).reshape(d, SSD_GROUPS * LANES)
    return jnp.concatenate([w[:, :dt0], w[:, u0:], w_dt], axis=1).astype(BF16)


def _group_lanes(p):
    n = p.shape[0]
    p = jnp.transpose(p.reshape(n, 2, SSD_GROUPS, SSD_RANK), (0, 2, 1, 3)).reshape(n, SSD_GROUPS, 2 * SSD_RANK)
    return jnp.pad(p, ((0, 0), (0, 0), (0, LANES - 2 * SSD_RANK)))


def kernel(x, ffn1_norm, ffn1_w_gate, ffn1_w_up, ffn1_w_down, mix_norm, ffn2_norm, ffn2_w_gate, ffn2_w_up,
           ffn2_w_down, even_w_in, ssd_conv_w, ssd_conv_b, ssd_dt_bias, ssd_a_log, ssd_d, ssd_norm, fourier_w,
           even_w_out, sgu_w_uv, sgu_b_uv, sgu_norm, sgu_w_s, sgu_b_s, odd_w_out, final_norm):
    bsz, seq, d = x.shape
    assert bsz == 1 and d == D_MODEL
    depth = ffn1_norm.shape[0]
    n_even = even_w_in.shape[0]
    xs = x.reshape(seq, d)

    row3 = lambda a: a.reshape(a.shape[0], 1, a.shape[1])
    f1 = (row3(ffn1_norm), ffn1_w_gate.astype(BF16), ffn1_w_up.astype(BF16), ffn1_w_down.astype(BF16))
    f2 = (row3(ffn2_norm), ffn2_w_gate.astype(BF16), ffn2_w_up.astype(BF16), ffn2_w_down.astype(BF16))
    mixn = row3(mix_norm)
    w_in = [_even_in_weight(even_w_in[j]) for j in range(n_even)]
    zeros = jnp.zeros((n_even, SSD_GROUPS, SUBLANES - 2, LANES), F32)
    par = jnp.concatenate([_group_lanes(ssd_dt_bias)[:, :, None, :], _group_lanes(ssd_a_log)[:, :, None, :], zeros],
                          axis=2)
    dskip = jnp.repeat(ssd_d, SSD_HEAD_DIM, axis=1).reshape(n_even, SSD_GROUPS, 1, SSD_GW)
    ssd_nw = ssd_norm.reshape(n_even, SSD_GROUPS, 1, SSD_GW)
    conv_b = row3(ssd_conv_b)
    w_out_even = even_w_out.astype(BF16)
    w_uv = sgu_w_uv.astype(BF16)
    b_uv = row3(sgu_b_uv)
    sgu_nw = row3(sgu_norm)
    w_s = sgu_w_s.astype(BF16)
    b_s = jnp.broadcast_to(sgu_b_s[..., None], sgu_b_s.shape + (LANES,))
    w_out_odd = odd_w_out.astype(BF16)

    for i in range(depth):
        xs = _ffn(xs, *f1, i)
        j = i // 2
        if i % 2 == 0:
            zx, u, dt_all = _inproj(xs, mixn, w_in[j], i)
            xbc = _conv(zx, ssd_conv_w, conv_b, j)
            y_ssd = _ssd(xbc, zx, dt_all, par, dskip, ssd_nw, j)
            fa, fb = _fourier_weights(fourier_w, j)
            y_fft = _dft_b(_dft_a(u), fa, fb, seq)
            xs = _outproj(xs, [y_ssd, y_fft], w_out_even, j)
        else:
            gated = _sgu(xs, mixn, w_uv, b_uv, sgu_nw, w_s, b_s, i, j)
            xs = _outproj(xs, [gated], w_out_odd, j)
        xs = _ffn(xs, *f2, i)
    return _final_norm(xs, final_norm.reshape(1, d)).reshape(bsz, seq, d)
```

```python
import functools

import numpy as np
import jax
import jax.numpy as jnp
from jax import lax
from jax.experimental import pallas as pl
from jax.experimental.pallas import tpu as pltpu

F32 = jnp.float32
BF16 = jnp.bfloat16
HIGHEST = lax.Precision.HIGHEST
LOG2E = 1.4426950408889634

D_MODEL = 2048
D_FF = 5632
EPS = 1e-6
SSD_HEAD_DIM = 64
SSD_HEADS = 48
SSD_GROUPS = 8
SSD_RANK = SSD_HEADS // SSD_GROUPS
SSD_GW = SSD_RANK * SSD_HEAD_DIM
SSD_STATE = 128
SSD_CHUNK = 256
SSD_WIDTH = SSD_HEADS * SSD_HEAD_DIM
SSD_CONV = 5
SSD_CONV_CH = SSD_WIDTH + 2 * SSD_GROUPS * SSD_STATE
FOURIER_WIDTH = 1024
FOURIER_GROUPS = 4
FOURIER_GD = FOURIER_WIDTH // FOURIER_GROUPS
SGU_WIDTH = 4096
SGU_GROUPS = 8
SGU_GD = SGU_WIDTH // SGU_GROUPS
SGU_CHUNK = 128
ZX_WIDTH = SSD_WIDTH + SSD_CONV_CH

LANES = 128
SUBLANES = 8
VMEM_LIMIT = 56 * 1024 * 1024

TM = 512
TMX = 1024
TF = 512
TN = 1024
DFT_N1 = 128
DFT_KB = 8


def _cparams(*sem):
    return pltpu.CompilerParams(dimension_semantics=sem, vmem_limit_bytes=VMEM_LIMIT)


def _rmsnorm(x, w):
    ms = jnp.mean(x * x, axis=-1, keepdims=True)
    return x * lax.rsqrt(ms + EPS) * w


def _ffn_body(x_ref, nw_ref, wg_ref, wu_ref, wd_ref, *rest):
    out_nw_ref = rest[0] if len(rest) == 3 else None
    o_ref, xn_ref = rest[-2:]
    j = pl.program_id(1)

    @pl.when(j == 0)
    def _():
        x = x_ref[...]
        xn_ref[...] = _rmsnorm(x, nw_ref[...]).astype(BF16)
        o_ref[...] = x

    xn = xn_ref[...]
    g = jnp.dot(xn, wg_ref[...], preferred_element_type=F32)
    u = jnp.dot(xn, wu_ref[...], preferred_element_type=F32)
    h = ((0.5 * g) * jax.nn.sigmoid(g) * u).astype(BF16)
    o_ref[...] += jnp.dot(h, wd_ref[...], preferred_element_type=F32)

    if out_nw_ref is not None:
        @pl.when(j == pl.num_programs(1) - 1)
        def _():
            o_ref[...] = _rmsnorm(o_ref[...], out_nw_ref[...])


def _ffn(x, nw, wg, wu, wd, layer, out_norm=None):
    seq = x.shape[0]
    tm = min(TMX, seq)
    in_specs = [
        pl.BlockSpec((tm, D_MODEL), lambda i, j: (i, 0)),
        pl.BlockSpec((None, 1, D_MODEL), lambda i, j: (layer, 0, 0)),
        pl.BlockSpec((None, D_MODEL, TF), lambda i, j: (layer, 0, j)),
        pl.BlockSpec((None, D_MODEL, TF), lambda i, j: (layer, 0, j)),
        pl.BlockSpec((None, TF, D_MODEL), lambda i, j: (layer, j, 0)),
    ]
    operands = [x, nw, wg, wu, wd]
    if out_norm is not None:
        in_specs.append(pl.BlockSpec((1, D_MODEL), lambda i, j: (0, 0)))
        operands.append(out_norm)
    return pl.pallas_call(
        _ffn_body,
        grid=(seq // tm, D_FF // TF),
        in_specs=in_specs,
        out_specs=pl.BlockSpec((tm, D_MODEL), lambda i, j: (i, 0)),
        out_shape=jax.ShapeDtypeStruct((seq, D_MODEL), F32),
        scratch_shapes=[pltpu.VMEM((tm, D_MODEL), BF16)],
        compiler_params=_cparams("parallel", "arbitrary"),
        name="ffn",
    )(*operands)


N_ZX_TILES = ZX_WIDTH // TN
IN_TILES = N_ZX_TILES + 2


def _inproj_body(x_ref, nw_ref, w_ref, wu_ref, wdt_ref, zx_ref, u_ref, dt_ref, h_ref):
    j = pl.program_id(1)

    @pl.when(j == 0)
    def _():
        h_ref[...] = _rmsnorm(x_ref[...], nw_ref[...]).astype(BF16)

    @pl.when(j < N_ZX_TILES)
    def _():
        zx_ref[...] = jnp.dot(h_ref[...], w_ref[...], preferred_element_type=F32).astype(BF16)

    @pl.when(j == N_ZX_TILES)
    def _():
        u_ref[...] = jnp.dot(h_ref[...], wu_ref[...], preferred_element_type=F32).astype(BF16)

    @pl.when(j == N_ZX_TILES + 1)
    def _():
        dt_ref[...] = jnp.dot(h_ref[...], wdt_ref[...], preferred_element_type=F32)


def _inproj(x, nw, w, w_u, w_dt, layer, j_even):
    seq = x.shape[0]
    tm = min(TMX, seq)
    once = pl.Buffered(1)
    return pl.pallas_call(
        _inproj_body,
        grid=(seq // tm, IN_TILES),
        in_specs=[
            pl.BlockSpec((tm, D_MODEL), lambda i, j: (i, 0), pipeline_mode=once),
            pl.BlockSpec((None, 1, D_MODEL), lambda i, j: (layer, 0, 0)),
            pl.BlockSpec((None, D_MODEL, TN), lambda i, j: (j_even, 0, jnp.minimum(j, N_ZX_TILES - 1))),
            pl.BlockSpec((None, D_MODEL, FOURIER_WIDTH), lambda i, j: (j_even, 0, 0), pipeline_mode=once),
            pl.BlockSpec((None, D_MODEL, SSD_GROUPS * LANES), lambda i, j: (j_even, 0, 0), pipeline_mode=once),
        ],
        out_specs=[
            pl.BlockSpec((tm, TN), lambda i, j: (i, jnp.minimum(j, N_ZX_TILES - 1))),
            pl.BlockSpec((tm, FOURIER_WIDTH), lambda i, j: (i, 0)),
            pl.BlockSpec((tm, SSD_GROUPS * LANES), lambda i, j: (i, 0)),
        ],
        out_shape=[
            jax.ShapeDtypeStruct((seq, ZX_WIDTH), BF16),
            jax.ShapeDtypeStruct((seq, FOURIER_WIDTH), BF16),
            jax.ShapeDtypeStruct((seq, SSD_GROUPS * LANES), F32),
        ],
        scratch_shapes=[pltpu.VMEM((tm, D_MODEL), BF16)],
        compiler_params=_cparams("parallel", "arbitrary"),
        name="even_inproj",
    )(x, nw, w, w_u, w_dt)


CONV_TR = 1024
CONV_TC = 512
CONV_HALO = 16


def _conv_body(xm_ref, xp_ref, xn_ref, w_ref, b_ref, o_ref):
    i = pl.program_id(0)
    last = pl.num_programs(0) - 1
    tr = xm_ref.shape[0]
    prev = xp_ref[...].astype(F32)[CONV_HALO - SUBLANES:]
    nxt = xn_ref[...].astype(F32)[:SUBLANES]
    ext = jnp.concatenate([jnp.where(i == 0, 0.0, prev), xm_ref[...].astype(F32), jnp.where(i == last, 0.0, nxt)],
                          axis=0)
    n = tr + 2 * SUBLANES
    w = w_ref[...]
    acc = jnp.broadcast_to(b_ref[...], o_ref.shape)
    half = SSD_CONV // 2
    for k in range(SSD_CONV):
        shifted = ext if k == half else pltpu.roll(ext, (half - k) % n, axis=0)
        acc = acc + shifted[SUBLANES:SUBLANES + tr, :] * w[k:k + 1, :]
    o_ref[...] = (acc * jax.nn.sigmoid(acc)).astype(BF16)


def _conv(zx, conv_w, conv_b, layer):
    seq = zx.shape[0]
    tr = min(CONV_TR, seq)
    col0 = SSD_WIDTH // CONV_TC
    hb = tr // CONV_HALO
    nhb = seq // CONV_HALO
    return pl.pallas_call(
        _conv_body,
        grid=(seq // tr, SSD_CONV_CH // CONV_TC),
        in_specs=[
            pl.BlockSpec((tr, CONV_TC), lambda i, j: (i, col0 + j)),
            pl.BlockSpec((CONV_HALO, CONV_TC), lambda i, j: (jnp.maximum(i * hb - 1, 0), col0 + j)),
            pl.BlockSpec((CONV_HALO, CONV_TC), lambda i, j: (jnp.minimum((i + 1) * hb, nhb - 1), col0 + j)),
            pl.BlockSpec((None, SSD_CONV, CONV_TC), lambda i, j: (layer, 0, j)),
            pl.BlockSpec((None, 1, CONV_TC), lambda i, j: (layer, 0, j)),
        ],
        out_specs=pl.BlockSpec((tr, CONV_TC), lambda i, j: (i, j)),
        out_shape=jax.ShapeDtypeStruct((seq, SSD_CONV_CH), BF16),
        compiler_params=_cparams("parallel", "parallel"),
        name="ssd_conv",
    )(zx, zx, zx, conv_w, conv_b)


def _softplus(v):
    return jnp.maximum(v, 0.0) + jnp.log1p(jnp.exp(-jnp.abs(v)))


def _head_rows(rows, first):
    n = rows.shape[1]
    return jnp.concatenate([jnp.broadcast_to(rows[first + h:first + h + 1, :], (SSD_HEAD_DIM, n))
                            for h in range(SSD_RANK)], axis=0)


def _ssd_body(x_ref, b_ref, c_ref, z_ref, dt_ref, par_ref, dskip_ref, nw_ref, tri_ref,
              o_ref, sf_ref, sb_ref, sball_ref, xt_ref, qn_ref, dtt_ref, qt_ref, tott_ref):
    phase = pl.program_id(1)
    c = pl.program_id(2)
    nc = pl.num_programs(2)
    t = SSD_CHUNK
    rk = SSD_RANK
    hd = SSD_HEAD_DIM
    prm = dtt_ref.shape[1]
    bm = b_ref[...]

    @pl.when(phase == 0)
    def _():
        @pl.when(c == 0)
        def _():
            sb_ref[...] = jnp.zeros_like(sb_ref)

        cc = nc - 1 - c
        par = par_ref[...]
        lane = lax.broadcasted_iota(jnp.int32, (1, LANES), 1)
        dt = _softplus(dt_ref[...] + par[0:1, :])
        da = dt * (-jnp.exp(par[1:2, :]))
        d1 = da.astype(BF16)
        r1 = da - d1.astype(F32)
        d2 = r1.astype(BF16)
        d3 = (r1 - d2.astype(F32)).astype(BF16)
        cs3 = jnp.dot(tri_ref[...], jnp.concatenate([d1, d2, d3], axis=1), preferred_element_type=F32)
        cs = cs3[:, :LANES] + cs3[:, LANES:2 * LANES] + cs3[:, 2 * LANES:]
        q = (cs - jnp.where(lane >= rk, da, 0.0)) * LOG2E
        dt_t = dt.T[:prm]
        q_t = q.T[:prm]
        tot_t = jnp.broadcast_to((cs * LOG2E).T[:prm, t - 1:t], (prm, t))
        x_t = x_ref[...].astype(F32).T
        xt_ref[cc] = x_t
        qn_ref[cc] = q
        dtt_ref[cc] = dt_t
        qt_ref[cc] = q_t
        tott_ref[cc] = tot_t

        sb = sb_ref[...]
        sball_ref[cc] = sb.astype(BF16)
        wb = dt_t * jnp.exp2(q_t)
        xw = (x_t * _head_rows(wb, rk)).astype(BF16)
        upd = jnp.dot(xw, bm, preferred_element_type=F32)
        sb_ref[...] = _head_rows(jnp.exp2(tot_t[:, :SSD_STATE]), rk) * sb + upd

    @pl.when(phase == 1)
    def _():
        @pl.when(c == 0)
        def _():
            sf_ref[...] = jnp.zeros_like(sf_ref)

        x_t = xt_ref[c]
        q = qn_ref[c]
        dt_t = dtt_ref[c]
        q_t = qt_ref[c]
        tot_t = tott_ref[c]
        cm = c_ref[...]
        gt = lax.dot_general(bm, cm, (((1,), (1,)), ((), ())), preferred_element_type=F32)
        srow = lax.broadcasted_iota(jnp.int32, (t, t), 0)
        tcol = lax.broadcasted_iota(jnp.int32, (t, t), 1)
        causal = srow <= tcol
        anti = srow >= tcol
        ys = []
        for h in range(rk):
            hb = rk + h
            xh = x_t[h * hd:(h + 1) * hd, :]
            lhs = jnp.concatenate([(xh * dt_t[h:h + 1, :]).astype(BF16), (xh * dt_t[hb:hb + 1, :]).astype(BF16)],
                                  axis=1)
            lf = jnp.where(causal, jnp.exp2(q_t[h:h + 1, :] - q[:, h:h + 1]), 0.0)
            lb = jnp.where(anti, jnp.exp2(q[:, hb:hb + 1] - q_t[hb:hb + 1, :]), 0.0)
            rhs = jnp.concatenate([(gt * lf).astype(BF16), (gt * lb).astype(BF16)], axis=0)
            ys.append(jnp.dot(lhs, rhs, preferred_element_type=F32))
        y_t = jnp.concatenate(ys, axis=0)

        sf = sf_ref[...]
        states = jnp.concatenate([sf.astype(BF16), sball_ref[c]], axis=0)
        off = lax.dot_general(states, cm, (((1,), (1,)), ((), ())), preferred_element_type=F32)
        y_t += off[:SSD_GW] * _head_rows(jnp.exp2(q_t), 0)
        y_t += off[SSD_GW:] * _head_rows(jnp.exp2(tot_t - q_t), rk)
        wf = dt_t * jnp.exp2(tot_t - q_t)
        xw = (x_t * _head_rows(wf, 0)).astype(BF16)
        upd = jnp.dot(xw, bm, preferred_element_type=F32)
        sf_ref[...] = _head_rows(jnp.exp2(tot_t[:, :SSD_STATE]), 0) * sf + upd

        y = y_t.T + dskip_ref[...] * x_ref[...].astype(F32)
        z = z_ref[...].astype(F32)
        y = y * (z * jax.nn.sigmoid(z))
        o_ref[...] = _rmsnorm(y, nw_ref[...]).astype(BF16)


def _ssd(xbc, zx, dt_all, par, dskip, nw, layer):
    seq = xbc.shape[0]
    nc = seq // SSD_CHUNK
    t = SSD_CHUNK
    b0 = SSD_WIDTH // SSD_STATE
    c0 = b0 + SSD_GROUPS
    tri = jnp.asarray(np.tril(np.ones((t, t), np.float32)), dtype=BF16)

    def cidx(p, c):
        return p * c + (1 - p) * (nc - 1 - c)

    return pl.pallas_call(
        _ssd_body,
        grid=(SSD_GROUPS, 2, nc),
        in_specs=[
            pl.BlockSpec((t, SSD_GW), lambda g, p, c: (cidx(p, c), g)),
            pl.BlockSpec((t, SSD_STATE), lambda g, p, c: (cidx(p, c), b0 + g)),
            pl.BlockSpec((t, SSD_STATE), lambda g, p, c: (p * c, c0 + g)),
            pl.BlockSpec((t, SSD_GW), lambda g, p, c: (p * c, g)),
            pl.BlockSpec((t, LANES), lambda g, p, c: ((1 - p) * (nc - 1 - c), g)),
            pl.BlockSpec((None, None, SUBLANES, LANES), lambda g, p, c: (layer, g, 0, 0)),
            pl.BlockSpec((None, None, 1, SSD_GW), lambda g, p, c: (layer, g, 0, 0)),
            pl.BlockSpec((None, None, 1, SSD_GW), lambda g, p, c: (layer, g, 0, 0)),
            pl.BlockSpec((t, t), lambda g, p, c: (0, 0)),
        ],
        out_specs=pl.BlockSpec((t, SSD_GW), lambda g, p, c: (p * c, g)),
        out_shape=jax.ShapeDtypeStruct((seq, SSD_WIDTH), BF16),
        scratch_shapes=[
            pltpu.VMEM((SSD_GW, SSD_STATE), F32),
            pltpu.VMEM((SSD_GW, SSD_STATE), F32),
            pltpu.VMEM((nc, SSD_GW, SSD_STATE), BF16),
            pltpu.VMEM((nc, SSD_GW, t), F32),
            pltpu.VMEM((nc, t, LANES), F32),
            pltpu.VMEM((nc, 2 * SUBLANES, t), F32),
            pltpu.VMEM((nc, 2 * SUBLANES, t), F32),
            pltpu.VMEM((nc, 2 * SUBLANES, t), F32),
        ],
        compiler_params=_cparams("arbitrary", "arbitrary", "arbitrary"),
        name="ssd_scan",
    )(xbc, xbc, xbc, zx, dt_all, par, dskip, nw, tri)


def _fw_body(cd_ref, sd_ref, w_ref, a_ref, b_ref):
    w = w_ref[...]
    a_ref[...] = jnp.dot(cd_ref[...], w, preferred_element_type=F32, precision=HIGHEST).astype(BF16)
    b_ref[...] = jnp.dot(sd_ref[...], w, preferred_element_type=F32, precision=HIGHEST).astype(BF16)


def _fourier_weights(fourier_w, layer):
    d = FOURIER_GD
    ang = 2.0 * np.pi * np.outer(np.arange(d), np.arange(d)) / d
    cd = jnp.asarray((np.cos(ang) / np.sqrt(d)).astype(np.float32))
    sd = jnp.asarray((np.sin(ang) / np.sqrt(d)).astype(np.float32))
    return pl.pallas_call(
        _fw_body,
        grid=(FOURIER_GROUPS,),
        in_specs=[
            pl.BlockSpec((d, d), lambda g: (0, 0)),
            pl.BlockSpec((d, d), lambda g: (0, 0)),
            pl.BlockSpec((None, None, d, d), lambda g: (layer, g, 0, 0)),
        ],
        out_specs=[pl.BlockSpec((None, d, d), lambda g: (g, 0, 0))] * 2,
        out_shape=[jax.ShapeDtypeStruct((FOURIER_GROUPS, d, d), BF16)] * 2,
        compiler_params=_cparams("parallel"),
        name="fourier_weights",
    )(cd, sd, fourier_w)


DFT_NB = 4


def _dft_a_body(x_ref, f_ref, tc_ref, ts_ref, o_ref):
    n1 = DFT_N1
    y = jnp.dot(f_ref[...], x_ref[...], preferred_element_type=F32)
    reps = FOURIER_WIDTH // LANES
    for b in range(tc_ref.shape[0]):
        sl = slice(b * FOURIER_WIDTH, (b + 1) * FOURIER_WIDTH)
        yr = y[:n1, sl]
        yi = y[n1:, sl]
        tc = jnp.tile(tc_ref[b], (1, reps))
        ts = jnp.tile(ts_ref[b], (1, reps))
        o_ref[:n1, sl] = (yr * tc + yi * ts).astype(BF16)
        o_ref[n1:, sl] = (yi * tc - yr * ts).astype(BF16)


def _dft_a(u):
    seq = u.shape[0]
    n1 = DFT_N1
    n2 = seq // n1
    nb = min(DFT_NB, n2)
    ang1 = 2.0 * np.pi * np.outer(np.arange(n1), np.arange(n1)) / n1
    f1 = jnp.asarray(np.concatenate([np.cos(ang1), -np.sin(ang1)], axis=0), dtype=BF16)
    angt = 2.0 * np.pi * np.outer(np.arange(n2), np.arange(n1)) / seq
    tc = jnp.asarray(np.repeat(np.cos(angt)[:, :, None], LANES, axis=2).astype(np.float32))
    ts = jnp.asarray(np.repeat(np.sin(angt)[:, :, None], LANES, axis=2).astype(np.float32))
    x2 = u.reshape(n1, n2 * FOURIER_WIDTH)
    return pl.pallas_call(
        _dft_a_body,
        grid=(n2 // nb,),
        in_specs=[
            pl.BlockSpec((n1, nb * FOURIER_WIDTH), lambda j: (0, j)),
            pl.BlockSpec((2 * n1, n1), lambda j: (0, 0)),
            pl.BlockSpec((nb, n1, LANES), lambda j: (j, 0, 0)),
            pl.BlockSpec((nb, n1, LANES), lambda j: (j, 0, 0)),
        ],
        out_specs=pl.BlockSpec((2 * n1, nb * FOURIER_WIDTH), lambda j: (0, j)),
        out_shape=jax.ShapeDtypeStruct((2 * n1, n2 * FOURIER_WIDTH), BF16),
        compiler_params=_cparams("parallel"),
        name="dft_stage_a",
    )(x2, f1, tc, ts)


def _dft_b_body(yr_ref, yi_ref, lr_ref, li_ref, a_ref, b_ref, o_ref):
    kb, n2, width = yr_ref.shape
    rhs = jnp.concatenate([yr_ref[...].reshape(kb * n2, width), yi_ref[...].reshape(kb * n2, width)], axis=0)
    zr = jnp.dot(lr_ref[...], rhs, preferred_element_type=F32).astype(BF16)
    zi = jnp.dot(li_ref[...], rhs, preferred_element_type=F32).astype(BF16)
    outs = []
    for g in range(FOURIER_GROUPS):
        sl = slice(g * FOURIER_GD, (g + 1) * FOURIER_GD)
        outs.append(jnp.dot(zr[:, sl], a_ref[g], preferred_element_type=F32)
                    + jnp.dot(zi[:, sl], b_ref[g], preferred_element_type=F32))
    o_ref[...] = jnp.concatenate(outs, axis=1).reshape(o_ref.shape)


def _dft_b(ya, fa, fb, seq):
    n1 = DFT_N1
    n2 = seq // n1
    kb = DFT_KB
    ang2 = 2.0 * np.pi * np.outer(np.arange(n2), np.arange(n2)) / n2
    c2 = np.cos(ang2) / np.sqrt(seq)
    s2 = np.sin(ang2) / np.sqrt(seq)
    eye = np.eye(kb)
    lr = np.concatenate([np.einsum('ab,kn->kabn', eye, c2).reshape(n2 * kb, kb * n2),
                         np.einsum('ab,kn->kabn', eye, s2).reshape(n2 * kb, kb * n2)], axis=1)
    li = np.concatenate([np.einsum('ab,kn->kabn', eye, -s2).reshape(n2 * kb, kb * n2),
                         np.einsum('ab,kn->kabn', eye, c2).reshape(n2 * kb, kb * n2)], axis=1)
    y3 = ya.reshape(2 * n1, n2, FOURIER_WIDTH)
    nk = n1 // kb
    out = pl.pallas_call(
        _dft_b_body,
        grid=(nk,),
        in_specs=[
            pl.BlockSpec((kb, n2, FOURIER_WIDTH), lambda j: (j, 0, 0)),
            pl.BlockSpec((kb, n2, FOURIER_WIDTH), lambda j: (nk + j, 0, 0)),
            pl.BlockSpec((n2 * kb, 2 * kb * n2), lambda j: (0, 0)),
            pl.BlockSpec((n2 * kb, 2 * kb * n2), lambda j: (0, 0)),
            pl.BlockSpec((FOURIER_GROUPS, FOURIER_GD, FOURIER_GD), lambda j: (0, 0, 0)),
            pl.BlockSpec((FOURIER_GROUPS, FOURIER_GD, FOURIER_GD), lambda j: (0, 0, 0)),
        ],
        out_specs=pl.BlockSpec((n2, kb, FOURIER_WIDTH), lambda j: (0, j, 0)),
        out_shape=jax.ShapeDtypeStruct((n2, n1, FOURIER_WIDTH), F32),
        compiler_params=_cparams("parallel"),
        name="dft_stage_b",
    )(y3, y3, jnp.asarray(lr, dtype=BF16), jnp.asarray(li, dtype=BF16), fa, fb)
    return out.reshape(seq, FOURIER_WIDTH)


def _outproj_body(*refs, n_lhs):
    x_ref = refs[0]
    lhs = refs[1:1 + n_lhs]
    ws = refs[1 + n_lhs:1 + 2 * n_lhs]
    o_ref = refs[1 + 2 * n_lhs]
    acc = x_ref[...]
    for a_ref, w_ref in zip(lhs, ws):
        acc = acc + jnp.dot(a_ref[...].astype(BF16), w_ref[...], preferred_element_type=F32)
    o_ref[...] = acc


def _outproj(x, lhs_list, w, layer):
    seq = x.shape[0]
    in_specs = [pl.BlockSpec((TM, TN), lambda j, i: (i, j))]
    for a in lhs_list:
        in_specs.append(pl.BlockSpec((TM, a.shape[1]), lambda j, i: (i, 0)))
    row = 0
    for a in lhs_list:
        k = a.shape[1]
        assert row % k == 0
        in_specs.append(pl.BlockSpec((None, k, TN), lambda j, i, rb=row // k: (layer, rb, j)))
        row += k
    return pl.pallas_call(
        functools.partial(_outproj_body, n_lhs=len(lhs_list)),
        grid=(D_MODEL // TN, seq // TM),
        in_specs=in_specs,
        out_specs=pl.BlockSpec((TM, TN), lambda j, i: (i, j)),
        out_shape=jax.ShapeDtypeStruct((seq, D_MODEL), F32),
        compiler_params=_cparams("parallel", "parallel"),
        name="outproj",
    )(x, *lhs_list, *([w] * len(lhs_list)))


SGU_V_TILES = SGU_WIDTH // TN
SGU_TILES = 2 * SGU_V_TILES


def _sgu_body(x_ref, nw_ref, w_ref, b_ref, vnw_ref, ws_ref, bs_ref, o_ref, h_ref, v_ref, ss_ref):
    j = pl.program_id(1)
    tm = x_ref.shape[0]

    @pl.when(j == 0)
    def _():
        h_ref[...] = _rmsnorm(x_ref[...], nw_ref[...]).astype(BF16)
        ss_ref[...] = jnp.zeros_like(ss_ref)

    acc = jnp.dot(h_ref[...], w_ref[...], preferred_element_type=F32) + b_ref[...]
    act = jax.nn.gelu(acc)

    @pl.when(j < SGU_V_TILES)
    def _():
        v_ref[j] = act.astype(BF16)
        ss_ref[...] += jnp.sum(act * act, axis=-1, keepdims=True)

    @pl.when(j == SGU_V_TILES - 1)
    def _():
        rs = lax.rsqrt(ss_ref[...] * (1.0 / SGU_WIDTH) + EPS)
        per_tile = TN // SGU_GD
        for g in range(SGU_GROUPS):
            tile, off = divmod(g, per_tile)
            sl = slice(off * SGU_GD, (off + 1) * SGU_GD)
            v = v_ref[tile, :, sl].astype(F32)
            v = (v * rs * vnw_ref[:, g * SGU_GD:(g + 1) * SGU_GD]).astype(BF16)
            bias = jnp.tile(bs_ref[g], (1, SGU_GD // LANES))
            for qc in range(tm // SGU_CHUNK):
                rows = slice(qc * SGU_CHUNK, (qc + 1) * SGU_CHUNK)
                mixed = jnp.dot(ws_ref[g], v[rows], preferred_element_type=F32) + bias
                v_ref[tile, rows, sl] = mixed.astype(BF16)

    @pl.when(j >= SGU_V_TILES)
    def _():
        o_ref[...] = (act * v_ref[j - SGU_V_TILES].astype(F32)).astype(BF16)


def _sgu(x, nw, w_uv, b_uv, vnw, w_s, b_s, layer, j_odd):
    seq = x.shape[0]
    tm = min(TMX, seq)
    wcol = lambda j: (j + SGU_V_TILES) % SGU_TILES
    return pl.pallas_call(
        _sgu_body,
        grid=(seq // tm, SGU_TILES),
        in_specs=[
            pl.BlockSpec((tm, D_MODEL), lambda i, j: (i, 0), pipeline_mode=pl.Buffered(1)),
            pl.BlockSpec((None, 1, D_MODEL), lambda i, j: (layer, 0, 0)),
            pl.BlockSpec((None, D_MODEL, TN), lambda i, j: (j_odd, 0, wcol(j))),
            pl.BlockSpec((None, 1, TN), lambda i, j: (j_odd, 0, wcol(j))),
            pl.BlockSpec((None, 1, SGU_WIDTH), lambda i, j: (j_odd, 0, 0)),
            pl.BlockSpec((None, SGU_GROUPS, SGU_CHUNK, SGU_CHUNK), lambda i, j: (j_odd, 0, 0, 0)),
            pl.BlockSpec((None, SGU_GROUPS, SGU_CHUNK, LANES), lambda i, j: (j_odd, 0, 0, 0)),
        ],
        out_specs=pl.BlockSpec((tm, TN), lambda i, j: (i, jnp.maximum(j - SGU_V_TILES, 0))),
        out_shape=jax.ShapeDtypeStruct((seq, SGU_WIDTH), BF16),
        scratch_shapes=[
            pltpu.VMEM((tm, D_MODEL), BF16),
            pltpu.VMEM((SGU_V_TILES, tm, TN), BF16),
            pltpu.VMEM((tm, 1), F32),
        ],
        compiler_params=_cparams("parallel", "arbitrary"),
        name="sgu",
    )(x, nw, w_uv, b_uv, vnw, w_s, b_s)


def _even_in_side_weights(w):
    n, d, _ = w.shape
    dt0 = ZX_WIDTH
    u0 = dt0 + 2 * SSD_HEADS
    w_dt = w[:, :, dt0:u0].reshape(n, d, 2, SSD_GROUPS, SSD_RANK)
    w_dt = jnp.transpose(w_dt, (0, 1, 3, 2, 4)).reshape(n, d, SSD_GROUPS, 2 * SSD_RANK)
    w_dt = jnp.pad(w_dt, ((0, 0), (0, 0), (0, 0), (0, LANES - 2 * SSD_RANK))).reshape(n, d, SSD_GROUPS * LANES)
    return w[:, :, u0:].astype(BF16), w_dt.astype(BF16)


def _group_lanes(p):
    n = p.shape[0]
    p = jnp.transpose(p.reshape(n, 2, SSD_GROUPS, SSD_RANK), (0, 2, 1, 3)).reshape(n, SSD_GROUPS, 2 * SSD_RANK)
    return jnp.pad(p, ((0, 0), (0, 0), (0, LANES - 2 * SSD_RANK)))


def kernel(x, ffn1_norm, ffn1_w_gate, ffn1_w_up, ffn1_w_down, mix_norm, ffn2_norm, ffn2_w_gate, ffn2_w_up,
           ffn2_w_down, even_w_in, ssd_conv_w, ssd_conv_b, ssd_dt_bias, ssd_a_log, ssd_d, ssd_norm, fourier_w,
           even_w_out, sgu_w_uv, sgu_b_uv, sgu_norm, sgu_w_s, sgu_b_s, odd_w_out, final_norm):
    bsz, seq, d = x.shape
    assert bsz == 1 and d == D_MODEL
    depth = ffn1_norm.shape[0]
    n_even = even_w_in.shape[0]
    xs = x.reshape(seq, d)

    row3 = lambda a: a.reshape(a.shape[0], 1, a.shape[1])
    f1 = (row3(ffn1_norm), ffn1_w_gate.astype(BF16), ffn1_w_up.astype(BF16), ffn1_w_down.astype(BF16))
    f2 = (row3(ffn2_norm), ffn2_w_gate.astype(BF16), ffn2_w_up.astype(BF16), ffn2_w_down.astype(BF16))
    mixn = row3(mix_norm)
    w_in = even_w_in.astype(BF16)
    w_in_u, w_in_dt = _even_in_side_weights(even_w_in)
    zeros = jnp.zeros((n_even, SSD_GROUPS, SUBLANES - 2, LANES), F32)
    par = jnp.concatenate([_group_lanes(ssd_dt_bias)[:, :, None, :], _group_lanes(ssd_a_log)[:, :, None, :], zeros],
                          axis=2)
    dskip = jnp.repeat(ssd_d, SSD_HEAD_DIM, axis=1).reshape(n_even, SSD_GROUPS, 1, SSD_GW)
    ssd_nw = ssd_norm.reshape(n_even, SSD_GROUPS, 1, SSD_GW)
    conv_b = row3(ssd_conv_b)
    w_out_even = even_w_out.astype(BF16)
    w_uv = sgu_w_uv.astype(BF16)
    b_uv = row3(sgu_b_uv)
    sgu_nw = row3(sgu_norm)
    w_s = sgu_w_s.astype(BF16)
    b_s = jnp.broadcast_to(sgu_b_s[..., None], sgu_b_s.shape + (LANES,))
    w_out_odd = odd_w_out.astype(BF16)

    for i in range(depth):
        xs = _ffn(xs, *f1, i)
        j = i // 2
        if i % 2 == 0:
            zx, u, dt_all = _inproj(xs, mixn, w_in, w_in_u, w_in_dt, i, j)
            xbc = _conv(zx, ssd_conv_w, conv_b, j)
            y_ssd = _ssd(xbc, zx, dt_all, par, dskip, ssd_nw, j)
            fa, fb = _fourier_weights(fourier_w, j)
            y_fft = _dft_b(_dft_a(u), fa, fb, seq)
            xs = _outproj(xs, [y_ssd, y_fft], w_out_even, j)
        else:
            gated = _sgu(xs, mixn, w_uv, b_uv, sgu_nw, w_s, b_s, i, j)
            xs = _outproj(xs, [gated], w_out_odd, j)
        xs = _ffn(xs, *f2, i, out_norm=final_norm.reshape(1, d) if i == depth - 1 else None)
    return xs.reshape(bsz, seq, d)
```

```python
import functools

import numpy as np
import jax
import jax.numpy as jnp
from jax import lax
from jax.experimental import pallas as pl
from jax.experimental.pallas import tpu as pltpu

F32 = jnp.float32
BF16 = jnp.bfloat16
HIGHEST = lax.Precision.HIGHEST
LOG2E = 1.4426950408889634

D_MODEL = 2048
D_FF = 5632
EPS = 1e-6
SSD_HEAD_DIM = 64
SSD_HEADS = 48
SSD_GROUPS = 8
SSD_RANK = SSD_HEADS // SSD_GROUPS
SSD_GW = SSD_RANK * SSD_HEAD_DIM
SSD_STATE = 128
SSD_CHUNK = 256
SSD_WIDTH = SSD_HEADS * SSD_HEAD_DIM
SSD_CONV = 5
SSD_CONV_CH = SSD_WIDTH + 2 * SSD_GROUPS * SSD_STATE
FOURIER_WIDTH = 1024
FOURIER_GROUPS = 4
FOURIER_GD = FOURIER_WIDTH // FOURIER_GROUPS
SGU_WIDTH = 4096
SGU_GROUPS = 8
SGU_GD = SGU_WIDTH // SGU_GROUPS
SGU_CHUNK = 128
ZX_WIDTH = SSD_WIDTH + SSD_CONV_CH

LANES = 128
SUBLANES = 8
VMEM_LIMIT = 56 * 1024 * 1024

TM = 512
TMX = 1024
TF = 512
TN = 1024
SSD_GPS = 2
DFT_N1 = 128
DFT_KB = 8


def _cparams(*sem):
    return pltpu.CompilerParams(dimension_semantics=sem, vmem_limit_bytes=VMEM_LIMIT)


def _rmsnorm(x, w):
    ms = jnp.mean(x * x, axis=-1, keepdims=True)
    return x * lax.rsqrt(ms + EPS) * w


def _ffn_body(x_ref, nw_ref, wg_ref, wu_ref, wd_ref, *rest):
    out_nw_ref = rest[0] if len(rest) == 3 else None
    o_ref, xn_ref = rest[-2:]
    j = pl.program_id(1)

    @pl.when(j == 0)
    def _():
        x = x_ref[...]
        xn_ref[...] = _rmsnorm(x, nw_ref[...]).astype(BF16)
        o_ref[...] = x

    xn = xn_ref[...]
    g = jnp.dot(xn, wg_ref[...], preferred_element_type=F32)
    u = jnp.dot(xn, wu_ref[...], preferred_element_type=F32)
    h = ((0.5 * g) * jax.nn.sigmoid(g) * u).astype(BF16)
    o_ref[...] += jnp.dot(h, wd_ref[...], preferred_element_type=F32)

    if out_nw_ref is not None:
        @pl.when(j == pl.num_programs(1) - 1)
        def _():
            o_ref[...] = _rmsnorm(o_ref[...], out_nw_ref[...])


def _ffn(x, nw, wg, wu, wd, layer, out_norm=None):
    seq = x.shape[0]
    tm = min(TMX, seq)
    in_specs = [
        pl.BlockSpec((tm, D_MODEL), lambda i, j: (i, 0)),
        pl.BlockSpec((None, 1, D_MODEL), lambda i, j: (layer, 0, 0)),
        pl.BlockSpec((None, D_MODEL, TF), lambda i, j: (layer, 0, j)),
        pl.BlockSpec((None, D_MODEL, TF), lambda i, j: (layer, 0, j)),
        pl.BlockSpec((None, TF, D_MODEL), lambda i, j: (layer, j, 0)),
    ]
    operands = [x, nw, wg, wu, wd]
    if out_norm is not None:
        in_specs.append(pl.BlockSpec((1, D_MODEL), lambda i, j: (0, 0)))
        operands.append(out_norm)
    return pl.pallas_call(
        _ffn_body,
        grid=(seq // tm, D_FF // TF),
        in_specs=in_specs,
        out_specs=pl.BlockSpec((tm, D_MODEL), lambda i, j: (i, 0)),
        out_shape=jax.ShapeDtypeStruct((seq, D_MODEL), F32),
        scratch_shapes=[pltpu.VMEM((tm, D_MODEL), BF16)],
        compiler_params=_cparams("parallel", "arbitrary"),
        name="ffn",
    )(*operands)


N_ZX_TILES = ZX_WIDTH // TN
IN_TILES = N_ZX_TILES + 2


def _inproj_body(x_ref, nw_ref, w_ref, wu_ref, wdt_ref, zx_ref, u_ref, dt_ref, h_ref):
    j = pl.program_id(1)

    @pl.when(j == 0)
    def _():
        h_ref[...] = _rmsnorm(x_ref[...], nw_ref[...]).astype(BF16)

    @pl.when(j < N_ZX_TILES)
    def _():
        zx_ref[...] = jnp.dot(h_ref[...], w_ref[...], preferred_element_type=F32).astype(BF16)

    @pl.when(j == N_ZX_TILES)
    def _():
        u_ref[...] = jnp.dot(h_ref[...], wu_ref[...], preferred_element_type=F32).astype(BF16)

    @pl.when(j == N_ZX_TILES + 1)
    def _():
        dt_ref[...] = jnp.dot(h_ref[...], wdt_ref[...], preferred_element_type=F32)


def _inproj(x, nw, w, w_u, w_dt, layer, j_even):
    seq = x.shape[0]
    tm = min(TMX, seq)
    once = pl.Buffered(1)
    return pl.pallas_call(
        _inproj_body,
        grid=(seq // tm, IN_TILES),
        in_specs=[
            pl.BlockSpec((tm, D_MODEL), lambda i, j: (i, 0), pipeline_mode=once),
            pl.BlockSpec((None, 1, D_MODEL), lambda i, j: (layer, 0, 0)),
            pl.BlockSpec((None, D_MODEL, TN), lambda i, j: (j_even, 0, jnp.minimum(j, N_ZX_TILES - 1))),
            pl.BlockSpec((None, D_MODEL, FOURIER_WIDTH), lambda i, j: (j_even, 0, 0), pipeline_mode=once),
            pl.BlockSpec((None, D_MODEL, SSD_GROUPS * LANES), lambda i, j: (j_even, 0, 0), pipeline_mode=once),
        ],
        out_specs=[
            pl.BlockSpec((tm, TN), lambda i, j: (i, jnp.minimum(j, N_ZX_TILES - 1))),
            pl.BlockSpec((tm, FOURIER_WIDTH), lambda i, j: (i, 0)),
            pl.BlockSpec((tm, SSD_GROUPS * LANES), lambda i, j: (i, 0)),
        ],
        out_shape=[
            jax.ShapeDtypeStruct((seq, ZX_WIDTH), BF16),
            jax.ShapeDtypeStruct((seq, FOURIER_WIDTH), BF16),
            jax.ShapeDtypeStruct((seq, SSD_GROUPS * LANES), F32),
        ],
        scratch_shapes=[pltpu.VMEM((tm, D_MODEL), BF16)],
        compiler_params=_cparams("parallel", "arbitrary"),
        name="even_inproj",
    )(x, nw, w, w_u, w_dt)


CONV_TR = 1024
CONV_TC = 512
CONV_HALO = 16


def _conv_body(xm_ref, xp_ref, xn_ref, w_ref, b_ref, o_ref):
    i = pl.program_id(0)
    last = pl.num_programs(0) - 1
    tr = xm_ref.shape[0]
    prev = xp_ref[...].astype(F32)[CONV_HALO - SUBLANES:]
    nxt = xn_ref[...].astype(F32)[:SUBLANES]
    ext = jnp.concatenate([jnp.where(i == 0, 0.0, prev), xm_ref[...].astype(F32), jnp.where(i == last, 0.0, nxt)],
                          axis=0)
    n = tr + 2 * SUBLANES
    w = w_ref[...]
    acc = jnp.broadcast_to(b_ref[...], o_ref.shape)
    half = SSD_CONV // 2
    for k in range(SSD_CONV):
        shifted = ext if k == half else pltpu.roll(ext, (half - k) % n, axis=0)
        acc = acc + shifted[SUBLANES:SUBLANES + tr, :] * w[k:k + 1, :]
    o_ref[...] = (acc * jax.nn.sigmoid(acc)).astype(BF16)


def _conv(zx, conv_w, conv_b, layer):
    seq = zx.shape[0]
    tr = min(CONV_TR, seq)
    col0 = SSD_WIDTH // CONV_TC
    hb = tr // CONV_HALO
    nhb = seq // CONV_HALO
    return pl.pallas_call(
        _conv_body,
        grid=(seq // tr, SSD_CONV_CH // CONV_TC),
        in_specs=[
            pl.BlockSpec((tr, CONV_TC), lambda i, j: (i, col0 + j)),
            pl.BlockSpec((CONV_HALO, CONV_TC), lambda i, j: (jnp.maximum(i * hb - 1, 0), col0 + j)),
            pl.BlockSpec((CONV_HALO, CONV_TC), lambda i, j: (jnp.minimum((i + 1) * hb, nhb - 1), col0 + j)),
            pl.BlockSpec((None, SSD_CONV, CONV_TC), lambda i, j: (layer, 0, j)),
            pl.BlockSpec((None, 1, CONV_TC), lambda i, j: (layer, 0, j)),
        ],
        out_specs=pl.BlockSpec((tr, CONV_TC), lambda i, j: (i, j)),
        out_shape=jax.ShapeDtypeStruct((seq, SSD_CONV_CH), BF16),
        compiler_params=_cparams("parallel", "parallel"),
        name="ssd_conv",
    )(zx, zx, zx, conv_w, conv_b)


def _softplus(v):
    return jnp.maximum(v, 0.0) + jnp.log1p(jnp.exp(-jnp.abs(v)))


def _head_rows(rows, first):
    n = rows.shape[1]
    return jnp.concatenate([jnp.broadcast_to(rows[first + h:first + h + 1, :], (SSD_HEAD_DIM, n))
                            for h in range(SSD_RANK)], axis=0)


def _ssd_body(x_ref, b_ref, c_ref, z_ref, dt_ref, par_ref, dskip_ref, nw_ref, tri_ref, o_ref, *scratch):
    phase = pl.program_id(1)
    c = pl.program_id(2)
    nc = pl.num_programs(2)
    sf_ref, sb_ref = scratch[:2]

    def group(gi):
        cols = lambda w: slice(gi * w, (gi + 1) * w)
        return dict(x_ref=x_ref.at[:, cols(SSD_GW)], b_ref=b_ref.at[:, cols(SSD_STATE)],
                    c_ref=c_ref.at[:, cols(SSD_STATE)], z_ref=z_ref.at[:, cols(SSD_GW)],
                    dt_ref=dt_ref.at[:, cols(LANES)], par_ref=par_ref.at[gi], dskip_ref=dskip_ref.at[gi],
                    nw_ref=nw_ref.at[gi], tri_ref=tri_ref, o_ref=o_ref.at[:, cols(SSD_GW)],
                    scratch=[s.at[gi] for s in scratch])

    @pl.when(phase == 0)
    def _():
        @pl.when(c == 0)
        def _():
            sb_ref[...] = jnp.zeros_like(sb_ref)

        for gi in range(SSD_GPS):
            _ssd_prepare(nc - 1 - c, **group(gi))

    @pl.when(phase == 1)
    def _():
        @pl.when(c == 0)
        def _():
            sf_ref[...] = jnp.zeros_like(sf_ref)

        for gi in range(SSD_GPS):
            _ssd_emit(c, **group(gi))


def _ssd_prepare(cc, x_ref, b_ref, dt_ref, par_ref, tri_ref, scratch, **unused):
    _, sb_ref, sball_ref, xt_ref, qn_ref, dtt_ref, qt_ref, tott_ref = scratch
    t = SSD_CHUNK
    rk = SSD_RANK
    prm = dtt_ref.shape[1]
    bm = b_ref[...]
    par = par_ref[...]
    lane = lax.broadcasted_iota(jnp.int32, (1, LANES), 1)
    dt = _softplus(dt_ref[...] + par[0:1, :])
    da = dt * (-jnp.exp(par[1:2, :]))
    d1 = da.astype(BF16)
    r1 = da - d1.astype(F32)
    d2 = r1.astype(BF16)
    d3 = (r1 - d2.astype(F32)).astype(BF16)
    cs3 = jnp.dot(tri_ref[...], jnp.concatenate([d1, d2, d3], axis=1), preferred_element_type=F32)
    cs = cs3[:, :LANES] + cs3[:, LANES:2 * LANES] + cs3[:, 2 * LANES:]
    q = (cs - jnp.where(lane >= rk, da, 0.0)) * LOG2E
    dt_t = dt.T[:prm]
    q_t = q.T[:prm]
    tot_t = jnp.broadcast_to((cs * LOG2E).T[:prm, t - 1:t], (prm, t))
    x_t = x_ref[...].astype(F32).T
    xt_ref[cc] = x_t
    qn_ref[cc] = q
    dtt_ref[cc] = dt_t
    qt_ref[cc] = q_t
    tott_ref[cc] = tot_t

    sb = sb_ref[...]
    sball_ref[cc] = sb.astype(BF16)
    wb = dt_t * jnp.exp2(q_t)
    xw = (x_t * _head_rows(wb, rk)).astype(BF16)
    upd = jnp.dot(xw, bm, preferred_element_type=F32)
    sb_ref[...] = _head_rows(jnp.exp2(tot_t[:, :SSD_STATE]), rk) * sb + upd


def _ssd_emit(c, x_ref, b_ref, c_ref, z_ref, dskip_ref, nw_ref, o_ref, scratch, **unused):
    sf_ref, _, sball_ref, xt_ref, qn_ref, dtt_ref, qt_ref, tott_ref = scratch
    t = SSD_CHUNK
    rk = SSD_RANK
    hd = SSD_HEAD_DIM
    bm = b_ref[...]
    x_t = xt_ref[c]
    q = qn_ref[c]
    dt_t = dtt_ref[c]
    q_t = qt_ref[c]
    tot_t = tott_ref[c]
    cm = c_ref[...]
    gt = lax.dot_general(bm, cm, (((1,), (1,)), ((), ())), preferred_element_type=F32)
    srow = lax.broadcasted_iota(jnp.int32, (t, t), 0)
    tcol = lax.broadcasted_iota(jnp.int32, (t, t), 1)
    causal = srow <= tcol
    anti = srow >= tcol
    ys = []
    for h in range(rk):
        hb = rk + h
        xh = x_t[h * hd:(h + 1) * hd, :]
        lhs = jnp.concatenate([(xh * dt_t[h:h + 1, :]).astype(BF16), (xh * dt_t[hb:hb + 1, :]).astype(BF16)],
                              axis=1)
        lf = jnp.where(causal, jnp.exp2(q_t[h:h + 1, :] - q[:, h:h + 1]), 0.0)
        lb = jnp.where(anti, jnp.exp2(q[:, hb:hb + 1] - q_t[hb:hb + 1, :]), 0.0)
        rhs = jnp.concatenate([(gt * lf).astype(BF16), (gt * lb).astype(BF16)], axis=0)
        ys.append(jnp.dot(lhs, rhs, preferred_element_type=F32))
    y_t = jnp.concatenate(ys, axis=0)

    sf = sf_ref[...]
    states = jnp.concatenate([sf.astype(BF16), sball_ref[c]], axis=0)
    off = lax.dot_general(states, cm, (((1,), (1,)), ((), ())), preferred_element_type=F32)
    y_t += off[:SSD_GW] * _head_rows(jnp.exp2(q_t), 0)
    y_t += off[SSD_GW:] * _head_rows(jnp.exp2(tot_t - q_t), rk)
    wf = dt_t * jnp.exp2(tot_t - q_t)
    xw = (x_t * _head_rows(wf, 0)).astype(BF16)
    upd = jnp.dot(xw, bm, preferred_element_type=F32)
    sf_ref[...] = _head_rows(jnp.exp2(tot_t[:, :SSD_STATE]), 0) * sf + upd

    y = y_t.T + dskip_ref[...] * x_ref[...].astype(F32)
    z = z_ref[...].astype(F32)
    y = y * (z * jax.nn.sigmoid(z))
    o_ref[...] = _rmsnorm(y, nw_ref[...]).astype(BF16)


def _ssd(xbc, zx, dt_all, par, dskip, nw, layer):
    seq = xbc.shape[0]
    nc = seq // SSD_CHUNK
    t = SSD_CHUNK
    b0 = SSD_WIDTH // SSD_STATE
    c0 = b0 + SSD_GROUPS
    tri = jnp.asarray(np.tril(np.ones((t, t), np.float32)), dtype=BF16)

    def cidx(p, c):
        return p * c + (1 - p) * (nc - 1 - c)

    gps = SSD_GPS
    assert b0 % gps == 0 and c0 % gps == 0
    return pl.pallas_call(
        _ssd_body,
        grid=(SSD_GROUPS // gps, 2, nc),
        in_specs=[
            pl.BlockSpec((t, gps * SSD_GW), lambda g, p, c: (cidx(p, c), g)),
            pl.BlockSpec((t, gps * SSD_STATE), lambda g, p, c: (cidx(p, c), b0 // gps + g)),
            pl.BlockSpec((t, gps * SSD_STATE), lambda g, p, c: (p * c, c0 // gps + g)),
            pl.BlockSpec((t, gps * SSD_GW), lambda g, p, c: (p * c, g)),
            pl.BlockSpec((t, gps * LANES), lambda g, p, c: ((1 - p) * (nc - 1 - c), g)),
            pl.BlockSpec((None, gps, SUBLANES, LANES), lambda g, p, c: (layer, g, 0, 0)),
            pl.BlockSpec((None, gps, 1, SSD_GW), lambda g, p, c: (layer, g, 0, 0)),
            pl.BlockSpec((None, gps, 1, SSD_GW), lambda g, p, c: (layer, g, 0, 0)),
            pl.BlockSpec((t, t), lambda g, p, c: (0, 0)),
        ],
        out_specs=pl.BlockSpec((t, gps * SSD_GW), lambda g, p, c: (p * c, g)),
        out_shape=jax.ShapeDtypeStruct((seq, SSD_WIDTH), BF16),
        scratch_shapes=[
            pltpu.VMEM((gps, SSD_GW, SSD_STATE), F32),
            pltpu.VMEM((gps, SSD_GW, SSD_STATE), F32),
            pltpu.VMEM((gps, nc, SSD_GW, SSD_STATE), BF16),
            pltpu.VMEM((gps, nc, SSD_GW, t), F32),
            pltpu.VMEM((gps, nc, t, LANES), F32),
            pltpu.VMEM((gps, nc, 2 * SUBLANES, t), F32),
            pltpu.VMEM((gps, nc, 2 * SUBLANES, t), F32),
            pltpu.VMEM((gps, nc, 2 * SUBLANES, t), F32),
        ],
        compiler_params=_cparams("arbitrary", "arbitrary", "arbitrary"),
        name="ssd_scan",
    )(xbc, xbc, xbc, zx, dt_all, par, dskip, nw, tri)


def _fw_body(cd_ref, sd_ref, w_ref, a_ref, b_ref):
    w = w_ref[...]
    a_ref[...] = jnp.dot(cd_ref[...], w, preferred_element_type=F32, precision=HIGHEST).astype(BF16)
    b_ref[...] = jnp.dot(sd_ref[...], w, preferred_element_type=F32, precision=HIGHEST).astype(BF16)


def _fourier_weights(fourier_w, layer):
    d = FOURIER_GD
    ang = 2.0 * np.pi * np.outer(np.arange(d), np.arange(d)) / d
    cd = jnp.asarray((np.cos(ang) / np.sqrt(d)).astype(np.float32))
    sd = jnp.asarray((np.sin(ang) / np.sqrt(d)).astype(np.float32))
    return pl.pallas_call(
        _fw_body,
        grid=(FOURIER_GROUPS,),
        in_specs=[
            pl.BlockSpec((d, d), lambda g: (0, 0)),
            pl.BlockSpec((d, d), lambda g: (0, 0)),
            pl.BlockSpec((None, None, d, d), lambda g: (layer, g, 0, 0)),
        ],
        out_specs=[pl.BlockSpec((None, d, d), lambda g: (g, 0, 0))] * 2,
        out_shape=[jax.ShapeDtypeStruct((FOURIER_GROUPS, d, d), BF16)] * 2,
        compiler_params=_cparams("parallel"),
        name="fourier_weights",
    )(cd, sd, fourier_w)


DFT_NB = 4


def _dft_a_body(x_ref, f_ref, tc_ref, ts_ref, o_ref):
    n1 = DFT_N1
    y = jnp.dot(f_ref[...], x_ref[...], preferred_element_type=F32)
    reps = FOURIER_WIDTH // LANES
    for b in range(tc_ref.shape[0]):
        sl = slice(b * FOURIER_WIDTH, (b + 1) * FOURIER_WIDTH)
        yr = y[:n1, sl]
        yi = y[n1:, sl]
        tc = jnp.tile(tc_ref[b], (1, reps))
        ts = jnp.tile(ts_ref[b], (1, reps))
        o_ref[:n1, sl] = (yr * tc + yi * ts).astype(BF16)
        o_ref[n1:, sl] = (yi * tc - yr * ts).astype(BF16)


def _dft_a(u):
    seq = u.shape[0]
    n1 = DFT_N1
    n2 = seq // n1
    nb = min(DFT_NB, n2)
    ang1 = 2.0 * np.pi * np.outer(np.arange(n1), np.arange(n1)) / n1
    f1 = jnp.asarray(np.concatenate([np.cos(ang1), -np.sin(ang1)], axis=0), dtype=BF16)
    angt = 2.0 * np.pi * np.outer(np.arange(n2), np.arange(n1)) / seq
    tc = jnp.asarray(np.repeat(np.cos(angt)[:, :, None], LANES, axis=2).astype(np.float32))
    ts = jnp.asarray(np.repeat(np.sin(angt)[:, :, None], LANES, axis=2).astype(np.float32))
    x2 = u.reshape(n1, n2 * FOURIER_WIDTH)
    return pl.pallas_call(
        _dft_a_body,
        grid=(n2 // nb,),
        in_specs=[
            pl.BlockSpec((n1, nb * FOURIER_WIDTH), lambda j: (0, j)),
            pl.BlockSpec((2 * n1, n1), lambda j: (0, 0)),
            pl.BlockSpec((nb, n1, LANES), lambda j: (j, 0, 0)),
            pl.BlockSpec((nb, n1, LANES), lambda j: (j, 0, 0)),
        ],
        out_specs=pl.BlockSpec((2 * n1, nb * FOURIER_WIDTH), lambda j: (0, j)),
        out_shape=jax.ShapeDtypeStruct((2 * n1, n2 * FOURIER_WIDTH), BF16),
        compiler_params=_cparams("parallel"),
        name="dft_stage_a",
    )(x2, f1, tc, ts)


def _dft_b_body(yr_ref, yi_ref, lr_ref, li_ref, a_ref, b_ref, o_ref):
    kb, n2, width = yr_ref.shape
    rhs = jnp.concatenate([yr_ref[...].reshape(kb * n2, width), yi_ref[...].reshape(kb * n2, width)], axis=0)
    zr = jnp.dot(lr_ref[...], rhs, preferred_element_type=F32).astype(BF16)
    zi = jnp.dot(li_ref[...], rhs, preferred_element_type=F32).astype(BF16)
    outs = []
    for g in range(FOURIER_GROUPS):
        sl = slice(g * FOURIER_GD, (g + 1) * FOURIER_GD)
        outs.append(jnp.dot(zr[:, sl], a_ref[g], preferred_element_type=F32)
                    + jnp.dot(zi[:, sl], b_ref[g], preferred_element_type=F32))
    o_ref[...] = jnp.concatenate(outs, axis=1).reshape(o_ref.shape)


def _dft_b(ya, fa, fb, seq):
    n1 = DFT_N1
    n2 = seq // n1
    kb = DFT_KB
    ang2 = 2.0 * np.pi * np.outer(np.arange(n2), np.arange(n2)) / n2
    c2 = np.cos(ang2) / np.sqrt(seq)
    s2 = np.sin(ang2) / np.sqrt(seq)
    eye = np.eye(kb)
    lr = np.concatenate([np.einsum('ab,kn->kabn', eye, c2).reshape(n2 * kb, kb * n2),
                         np.einsum('ab,kn->kabn', eye, s2).reshape(n2 * kb, kb * n2)], axis=1)
    li = np.concatenate([np.einsum('ab,kn->kabn', eye, -s2).reshape(n2 * kb, kb * n2),
                         np.einsum('ab,kn->kabn', eye, c2).reshape(n2 * kb, kb * n2)], axis=1)
    y3 = ya.reshape(2 * n1, n2, FOURIER_WIDTH)
    nk = n1 // kb
    out = pl.pallas_call(
        _dft_b_body,
        grid=(nk,),
        in_specs=[
            pl.BlockSpec((kb, n2, FOURIER_WIDTH), lambda j: (j, 0, 0)),
            pl.BlockSpec((kb, n2, FOURIER_WIDTH), lambda j: (nk + j, 0, 0)),
            pl.BlockSpec((n2 * kb, 2 * kb * n2), lambda j: (0, 0)),
            pl.BlockSpec((n2 * kb, 2 * kb * n2), lambda j: (0, 0)),
            pl.BlockSpec((FOURIER_GROUPS, FOURIER_GD, FOURIER_GD), lambda j: (0, 0, 0)),
            pl.BlockSpec((FOURIER_GROUPS, FOURIER_GD, FOURIER_GD), lambda j: (0, 0, 0)),
        ],
        out_specs=pl.BlockSpec((n2, kb, FOURIER_WIDTH), lambda j: (0, j, 0)),
        out_shape=jax.ShapeDtypeStruct((n2, n1, FOURIER_WIDTH), F32),
        compiler_params=_cparams("parallel"),
        name="dft_stage_b",
    )(y3, y3, jnp.asarray(lr, dtype=BF16), jnp.asarray(li, dtype=BF16), fa, fb)
    return out.reshape(seq, FOURIER_WIDTH)


def _outproj_body(*refs, n_lhs):
    x_ref = refs[0]
    lhs = refs[1:1 + n_lhs]
    ws = refs[1 + n_lhs:1 + 2 * n_lhs]
    o_ref = refs[1 + 2 * n_lhs]
    acc = x_ref[...]
    for a_ref, w_ref in zip(lhs, ws):
        acc = acc + jnp.dot(a_ref[...].astype(BF16), w_ref[...], preferred_element_type=F32)
    o_ref[...] = acc


def _outproj(x, lhs_list, w, layer):
    seq = x.shape[0]
    in_specs = [pl.BlockSpec((TM, TN), lambda j, i: (i, j))]
    for a in lhs_list:
        in_specs.append(pl.BlockSpec((TM, a.shape[1]), lambda j, i: (i, 0)))
    row = 0
    for a in lhs_list:
        k = a.shape[1]
        assert row % k == 0
        in_specs.append(pl.BlockSpec((None, k, TN), lambda j, i, rb=row // k: (layer, rb, j)))
        row += k
    return pl.pallas_call(
        functools.partial(_outproj_body, n_lhs=len(lhs_list)),
        grid=(D_MODEL // TN, seq // TM),
        in_specs=in_specs,
        out_specs=pl.BlockSpec((TM, TN), lambda j, i: (i, j)),
        out_shape=jax.ShapeDtypeStruct((seq, D_MODEL), F32),
        compiler_params=_cparams("parallel", "parallel"),
        name="outproj",
    )(x, *lhs_list, *([w] * len(lhs_list)))


SGU_V_TILES = SGU_WIDTH // TN
SGU_TILES = 2 * SGU_V_TILES


def _sgu_body(x_ref, nw_ref, w_ref, b_ref, vnw_ref, ws_ref, bs_ref, o_ref, h_ref, v_ref, ss_ref):
    j = pl.program_id(1)
    tm = x_ref.shape[0]

    @pl.when(j == 0)
    def _():
        h_ref[...] = _rmsnorm(x_ref[...], nw_ref[...]).astype(BF16)
        ss_ref[...] = jnp.zeros_like(ss_ref)

    acc = jnp.dot(h_ref[...], w_ref[...], preferred_element_type=F32) + b_ref[...]
    act = jax.nn.gelu(acc)

    @pl.when(j < SGU_V_TILES)
    def _():
        v_ref[j] = act.astype(BF16)
        ss_ref[...] += jnp.sum(act * act, axis=-1, keepdims=True)

    @pl.when(j == SGU_V_TILES - 1)
    def _():
        rs = lax.rsqrt(ss_ref[...] * (1.0 / SGU_WIDTH) + EPS)
        per_tile = TN // SGU_GD
        for g in range(SGU_GROUPS):
            tile, off = divmod(g, per_tile)
            sl = slice(off * SGU_GD, (off + 1) * SGU_GD)
            v = v_ref[tile, :, sl].astype(F32)
            v = (v * rs * vnw_ref[:, g * SGU_GD:(g + 1) * SGU_GD]).astype(BF16)
            bias = jnp.tile(bs_ref[g], (1, SGU_GD // LANES))
            for qc in range(tm // SGU_CHUNK):
                rows = slice(qc * SGU_CHUNK, (qc + 1) * SGU_CHUNK)
                mixed = jnp.dot(ws_ref[g], v[rows], preferred_element_type=F32) + bias
                v_ref[tile, rows, sl] = mixed.astype(BF16)

    @pl.when(j >= SGU_V_TILES)
    def _():
        o_ref[...] = (act * v_ref[j - SGU_V_TILES].astype(F32)).astype(BF16)


def _sgu(x, nw, w_uv, b_uv, vnw, w_s, b_s, layer, j_odd):
    seq = x.shape[0]
    tm = min(TMX, seq)
    wcol = lambda j: (j + SGU_V_TILES) % SGU_TILES
    return pl.pallas_call(
        _sgu_body,
        grid=(seq // tm, SGU_TILES),
        in_specs=[
            pl.BlockSpec((tm, D_MODEL), lambda i, j: (i, 0)),
            pl.BlockSpec((None, 1, D_MODEL), lambda i, j: (layer, 0, 0)),
            pl.BlockSpec((None, D_MODEL, TN), lambda i, j: (j_odd, 0, wcol(j))),
            pl.BlockSpec((None, 1, TN), lambda i, j: (j_odd, 0, wcol(j))),
            pl.BlockSpec((None, 1, SGU_WIDTH), lambda i, j: (j_odd, 0, 0)),
            pl.BlockSpec((None, SGU_GROUPS, SGU_CHUNK, SGU_CHUNK), lambda i, j: (j_odd, 0, 0, 0)),
            pl.BlockSpec((None, SGU_GROUPS, SGU_CHUNK, LANES), lambda i, j: (j_odd, 0, 0, 0)),
        ],
        out_specs=pl.BlockSpec((tm, TN), lambda i, j: (i, jnp.maximum(j - SGU_V_TILES, 0))),
        out_shape=jax.ShapeDtypeStruct((seq, SGU_WIDTH), BF16),
        scratch_shapes=[
            pltpu.VMEM((tm, D_MODEL), BF16),
            pltpu.VMEM((SGU_V_TILES, tm, TN), BF16),
            pltpu.VMEM((tm, 1), F32),
        ],
        compiler_params=_cparams("parallel", "arbitrary"),
        name="sgu",
    )(x, nw, w_uv, b_uv, vnw, w_s, b_s)


def _even_in_side_weights(w):
    n, d, _ = w.shape
    dt0 = ZX_WIDTH
    u0 = dt0 + 2 * SSD_HEADS
    w_dt = w[:, :, dt0:u0].reshape(n, d, 2, SSD_GROUPS, SSD_RANK)
    w_dt = jnp.transpose(w_dt, (0, 1, 3, 2, 4)).reshape(n, d, SSD_GROUPS, 2 * SSD_RANK)
    w_dt = jnp.pad(w_dt, ((0, 0), (0, 0), (0, 0), (0, LANES - 2 * SSD_RANK))).reshape(n, d, SSD_GROUPS * LANES)
    return w[:, :, u0:].astype(BF16), w_dt.astype(BF16)


def _group_lanes(p):
    n = p.shape[0]
    p = jnp.transpose(p.reshape(n, 2, SSD_GROUPS, SSD_RANK), (0, 2, 1, 3)).reshape(n, SSD_GROUPS, 2 * SSD_RANK)
    return jnp.pad(p, ((0, 0), (0, 0), (0, LANES - 2 * SSD_RANK)))


def kernel(x, ffn1_norm, ffn1_w_gate, ffn1_w_up, ffn1_w_down, mix_norm, ffn2_norm, ffn2_w_gate, ffn2_w_up,
           ffn2_w_down, even_w_in, ssd_conv_w, ssd_conv_b, ssd_dt_bias, ssd_a_log, ssd_d, ssd_norm, fourier_w,
           even_w_out, sgu_w_uv, sgu_b_uv, sgu_norm, sgu_w_s, sgu_b_s, odd_w_out, final_norm):
    bsz, seq, d = x.shape
    assert bsz == 1 and d == D_MODEL
    depth = ffn1_norm.shape[0]
    n_even = even_w_in.shape[0]
    xs = x.reshape(seq, d)

    row3 = lambda a: a.reshape(a.shape[0], 1, a.shape[1])
    f1 = (row3(ffn1_norm), ffn1_w_gate.astype(BF16), ffn1_w_up.astype(BF16), ffn1_w_down.astype(BF16))
    f2 = (row3(ffn2_norm), ffn2_w_gate.astype(BF16), ffn2_w_up.astype(BF16), ffn2_w_down.astype(BF16))
    mixn = row3(mix_norm)
    w_in = even_w_in[:, :, :ZX_WIDTH].astype(BF16)
    w_in_u, w_in_dt = _even_in_side_weights(even_w_in)
    zeros = jnp.zeros((n_even, SSD_GROUPS, SUBLANES - 2, LANES), F32)
    par = jnp.concatenate([_group_lanes(ssd_dt_bias)[:, :, None, :], _group_lanes(ssd_a_log)[:, :, None, :], zeros],
                          axis=2)
    dskip = jnp.repeat(ssd_d, SSD_HEAD_DIM, axis=1).reshape(n_even, SSD_GROUPS, 1, SSD_GW)
    ssd_nw = ssd_norm.reshape(n_even, SSD_GROUPS, 1, SSD_GW)
    conv_b = row3(ssd_conv_b)
    w_out_even = even_w_out.astype(BF16)
    w_uv = sgu_w_uv.astype(BF16)
    b_uv = row3(sgu_b_uv)
    sgu_nw = row3(sgu_norm)
    w_s = sgu_w_s.astype(BF16)
    b_s = jnp.broadcast_to(sgu_b_s[..., None], sgu_b_s.shape + (LANES,))
    w_out_odd = odd_w_out.astype(BF16)

    for i in range(depth):
        xs = _ffn(xs, *f1, i)
        j = i // 2
        if i % 2 == 0:
            zx, u, dt_all = _inproj(xs, mixn, w_in, w_in_u, w_in_dt, i, j)
            xbc = _conv(zx, ssd_conv_w, conv_b, j)
            y_ssd = _ssd(xbc, zx, dt_all, par, dskip, ssd_nw, j)
            fa, fb = _fourier_weights(fourier_w, j)
            y_fft = _dft_b(_dft_a(u), fa, fb, seq)
            xs = _outproj(xs, [y_ssd, y_fft], w_out_even, j)
        else:
            gated = _sgu(xs, mixn, w_uv, b_uv, sgu_nw, w_s, b_s, i, j)
            xs = _outproj(xs, [gated], w_out_odd, j)
        xs = _ffn(xs, *f2, i, out_norm=final_norm.reshape(1, d) if i == depth - 1 else None)
    return xs.reshape(bsz, seq, d)
```

```python
import functools

import numpy as np
import jax
import jax.numpy as jnp
from jax import lax
from jax.experimental import pallas as pl
from jax.experimental.pallas import tpu as pltpu

F32 = jnp.float32
BF16 = jnp.bfloat16
HIGHEST = lax.Precision.HIGHEST
LOG2E = 1.4426950408889634

D_MODEL = 2048
D_FF = 5632
EPS = 1e-6
SSD_HEAD_DIM = 64
SSD_HEADS = 48
SSD_GROUPS = 8
SSD_RANK = SSD_HEADS // SSD_GROUPS
SSD_GW = SSD_RANK * SSD_HEAD_DIM
SSD_STATE = 128
SSD_CHUNK = 256
SSD_WIDTH = SSD_HEADS * SSD_HEAD_DIM
SSD_CONV = 5
SSD_CONV_CH = SSD_WIDTH + 2 * SSD_GROUPS * SSD_STATE
FOURIER_WIDTH = 1024
FOURIER_GROUPS = 4
FOURIER_GD = FOURIER_WIDTH // FOURIER_GROUPS
SGU_WIDTH = 4096
SGU_GROUPS = 8
SGU_GD = SGU_WIDTH // SGU_GROUPS
SGU_CHUNK = 128
ZX_WIDTH = SSD_WIDTH + SSD_CONV_CH

LANES = 128
SUBLANES = 8
VMEM_LIMIT = 56 * 1024 * 1024
VMEM_LIMIT_FFN = 60 * 1024 * 1024

TM = 512
TMX = 1024
TF = 512
TN = 1024
SSD_GPS = 2
DFT_N1 = 128
DFT_KB = 8


def _cparams(*sem, vmem_limit=VMEM_LIMIT):
    return pltpu.CompilerParams(dimension_semantics=sem, vmem_limit_bytes=vmem_limit)


def _rmsnorm(x, w):
    ms = jnp.mean(x * x, axis=-1, keepdims=True)
    return x * lax.rsqrt(ms + EPS) * w


def _ffn_body(*refs, cast_next, out_norm):
    x_ref, nw_ref, wg_ref, wu_ref, wd_ref = refs[:5]
    pos = 5
    nxt_in = refs[pos:pos + 3] if cast_next else ()
    pos += len(nxt_in)
    out_nw_ref = refs[pos] if out_norm else None
    pos += int(out_norm)
    o_ref = refs[pos]
    nxt_out = refs[pos + 1:pos + 1 + len(nxt_in)]
    xn_ref = refs[-1]
    j = pl.program_id(1)

    @pl.when(j == 0)
    def _():
        x = x_ref[...]
        xn_ref[...] = _rmsnorm(x, nw_ref[...]).astype(BF16)
        o_ref[...] = x

    xn = xn_ref[...]
    g = jnp.dot(xn, wg_ref[...], preferred_element_type=F32)
    u = jnp.dot(xn, wu_ref[...], preferred_element_type=F32)
    h = ((0.5 * g) * jax.nn.sigmoid(g) * u).astype(BF16)
    o_ref[...] += jnp.dot(h, wd_ref[...], preferred_element_type=F32)
    for src, dst in zip(nxt_in, nxt_out):
        dst[...] = src[...].astype(BF16)

    if out_norm:
        @pl.when(j == pl.num_programs(1) - 1)
        def _():
            o_ref[...] = _rmsnorm(o_ref[...], out_nw_ref[...])


def _ffn(x, nw, w, layer, next_w=None, out_norm=None):
    seq = x.shape[0]
    tm = min(TMX, seq)
    ni, nj = seq // tm, D_FF // TF
    in_specs = [
        pl.BlockSpec((tm, D_MODEL), lambda i, j: (i, 0)),
        pl.BlockSpec((None, 1, D_MODEL), lambda i, j: (layer, 0, 0)),
        pl.BlockSpec((D_MODEL, TF), lambda i, j: (0, j)),
        pl.BlockSpec((D_MODEL, TF), lambda i, j: (0, j)),
        pl.BlockSpec((TF, D_MODEL), lambda i, j: (j, 0)),
    ]
    operands = [x, nw, *w]
    out_specs = [pl.BlockSpec((tm, D_MODEL), lambda i, j: (i, 0))]
    out_shape = [jax.ShapeDtypeStruct((seq, D_MODEL), F32)]
    if next_w is not None:
        g32, u32, d32, nl = next_w
        assert D_MODEL % ni == 0
        rb = D_MODEL // ni
        in_specs += [pl.BlockSpec((None, rb, TF), lambda i, j: (nl, i, j)),
                     pl.BlockSpec((None, rb, TF), lambda i, j: (nl, i, j)),
                     pl.BlockSpec((None, TF, rb), lambda i, j: (nl, j, i))]
        operands += [g32, u32, d32]
        out_specs += [pl.BlockSpec((rb, TF), lambda i, j: (i, j)),
                      pl.BlockSpec((rb, TF), lambda i, j: (i, j)),
                      pl.BlockSpec((TF, rb), lambda i, j: (j, i))]
        out_shape += [jax.ShapeDtypeStruct((D_MODEL, D_FF), BF16),
                      jax.ShapeDtypeStruct((D_MODEL, D_FF), BF16),
                      jax.ShapeDtypeStruct((D_FF, D_MODEL), BF16)]
    if out_norm is not None:
        in_specs.append(pl.BlockSpec((1, D_MODEL), lambda i, j: (0, 0)))
        operands.append(out_norm)
    outs = pl.pallas_call(
        functools.partial(_ffn_body, cast_next=next_w is not None, out_norm=out_norm is not None),
        grid=(ni, nj),
        in_specs=in_specs,
        out_specs=out_specs,
        out_shape=out_shape,
        scratch_shapes=[pltpu.VMEM((tm, D_MODEL), BF16)],
        compiler_params=_cparams("parallel", "arbitrary", vmem_limit=VMEM_LIMIT_FFN),
        name="ffn",
    )(*operands)
    return outs[0], tuple(outs[1:])


N_ZX_TILES = ZX_WIDTH // TN
IN_TILES = N_ZX_TILES + 2


def _inproj_body(x_ref, nw_ref, w_ref, wu_ref, wdt_ref, zx_ref, u_ref, dt_ref, h_ref):
    j = pl.program_id(1)

    @pl.when(j == 0)
    def _():
        h_ref[...] = _rmsnorm(x_ref[...], nw_ref[...]).astype(BF16)

    @pl.when(j < N_ZX_TILES)
    def _():
        zx_ref[...] = jnp.dot(h_ref[...], w_ref[...], preferred_element_type=F32).astype(BF16)

    @pl.when(j == N_ZX_TILES)
    def _():
        u_ref[...] = jnp.dot(h_ref[...], wu_ref[...], preferred_element_type=F32).astype(BF16)

    @pl.when(j == N_ZX_TILES + 1)
    def _():
        dt_ref[...] = jnp.dot(h_ref[...], wdt_ref[...], preferred_element_type=F32)


def _inproj(x, nw, w, w_u, w_dt, layer, j_even):
    seq = x.shape[0]
    tm = min(TMX, seq)
    once = pl.Buffered(1)
    return pl.pallas_call(
        _inproj_body,
        grid=(seq // tm, IN_TILES),
        in_specs=[
            pl.BlockSpec((tm, D_MODEL), lambda i, j: (i, 0), pipeline_mode=once),
            pl.BlockSpec((None, 1, D_MODEL), lambda i, j: (layer, 0, 0)),
            pl.BlockSpec((None, D_MODEL, TN), lambda i, j: (j_even, 0, jnp.minimum(j, N_ZX_TILES - 1))),
            pl.BlockSpec((None, D_MODEL, FOURIER_WIDTH), lambda i, j: (j_even, 0, 0), pipeline_mode=once),
            pl.BlockSpec((None, D_MODEL, SSD_GROUPS * LANES), lambda i, j: (j_even, 0, 0), pipeline_mode=once),
        ],
        out_specs=[
            pl.BlockSpec((tm, TN), lambda i, j: (i, jnp.minimum(j, N_ZX_TILES - 1))),
            pl.BlockSpec((tm, FOURIER_WIDTH), lambda i, j: (i, 0)),
            pl.BlockSpec((tm, SSD_GROUPS * LANES), lambda i, j: (i, 0)),
        ],
        out_shape=[
            jax.ShapeDtypeStruct((seq, ZX_WIDTH), BF16),
            jax.ShapeDtypeStruct((seq, FOURIER_WIDTH), BF16),
            jax.ShapeDtypeStruct((seq, SSD_GROUPS * LANES), F32),
        ],
        scratch_shapes=[pltpu.VMEM((tm, D_MODEL), BF16)],
        compiler_params=_cparams("parallel", "arbitrary"),
        name="even_inproj",
    )(x, nw, w, w_u, w_dt)


CONV_TR = 1024
CONV_TC = 512
CONV_HALO = 16


def _conv_body(xm_ref, xp_ref, xn_ref, w_ref, b_ref, o_ref):
    i = pl.program_id(0)
    last = pl.num_programs(0) - 1
    tr = xm_ref.shape[0]
    prev = xp_ref[...].astype(F32)[CONV_HALO - SUBLANES:]
    nxt = xn_ref[...].astype(F32)[:SUBLANES]
    ext = jnp.concatenate([jnp.where(i == 0, 0.0, prev), xm_ref[...].astype(F32), jnp.where(i == last, 0.0, nxt)],
                          axis=0)
    n = tr + 2 * SUBLANES
    w = w_ref[...]
    acc = jnp.broadcast_to(b_ref[...], o_ref.shape)
    half = SSD_CONV // 2
    for k in range(SSD_CONV):
        shifted = ext if k == half else pltpu.roll(ext, (half - k) % n, axis=0)
        acc = acc + shifted[SUBLANES:SUBLANES + tr, :] * w[k:k + 1, :]
    o_ref[...] = (acc * jax.nn.sigmoid(acc)).astype(BF16)


def _conv(zx, conv_w, conv_b, layer):
    seq = zx.shape[0]
    tr = min(CONV_TR, seq)
    col0 = SSD_WIDTH // CONV_TC
    hb = tr // CONV_HALO
    nhb = seq // CONV_HALO
    return pl.pallas_call(
        _conv_body,
        grid=(seq // tr, SSD_CONV_CH // CONV_TC),
        in_specs=[
            pl.BlockSpec((tr, CONV_TC), lambda i, j: (i, col0 + j)),
            pl.BlockSpec((CONV_HALO, CONV_TC), lambda i, j: (jnp.maximum(i * hb - 1, 0), col0 + j)),
            pl.BlockSpec((CONV_HALO, CONV_TC), lambda i, j: (jnp.minimum((i + 1) * hb, nhb - 1), col0 + j)),
            pl.BlockSpec((None, SSD_CONV, CONV_TC), lambda i, j: (layer, 0, j)),
            pl.BlockSpec((None, 1, CONV_TC), lambda i, j: (layer, 0, j)),
        ],
        out_specs=pl.BlockSpec((tr, CONV_TC), lambda i, j: (i, j)),
        out_shape=jax.ShapeDtypeStruct((seq, SSD_CONV_CH), BF16),
        compiler_params=_cparams("parallel", "parallel"),
        name="ssd_conv",
    )(zx, zx, zx, conv_w, conv_b)


def _softplus(v):
    return jnp.maximum(v, 0.0) + jnp.log1p(jnp.exp(-jnp.abs(v)))


def _head_rows(rows, first):
    n = rows.shape[1]
    return jnp.concatenate([jnp.broadcast_to(rows[first + h:first + h + 1, :], (SSD_HEAD_DIM, n))
                            for h in range(SSD_RANK)], axis=0)


def _ssd_body(x_ref, b_ref, c_ref, z_ref, dt_ref, par_ref, dskip_ref, nw_ref, tri_ref, o_ref, *scratch):
    phase = pl.program_id(1)
    c = pl.program_id(2)
    nc = pl.num_programs(2)
    sf_ref, sb_ref = scratch[:2]

    def group(gi):
        cols = lambda w: slice(gi * w, (gi + 1) * w)
        return dict(x_ref=x_ref.at[:, cols(SSD_GW)], b_ref=b_ref.at[:, cols(SSD_STATE)],
                    c_ref=c_ref.at[:, cols(SSD_STATE)], z_ref=z_ref.at[:, cols(SSD_GW)],
                    dt_ref=dt_ref.at[:, cols(LANES)], par_ref=par_ref.at[gi], dskip_ref=dskip_ref.at[gi],
                    nw_ref=nw_ref.at[gi], tri_ref=tri_ref, o_ref=o_ref.at[:, cols(SSD_GW)],
                    scratch=[s.at[gi] for s in scratch])

    @pl.when(phase == 0)
    def _():
        @pl.when(c == 0)
        def _():
            sb_ref[...] = jnp.zeros_like(sb_ref)

        for gi in range(SSD_GPS):
            _ssd_prepare(nc - 1 - c, **group(gi))

    @pl.when(phase == 1)
    def _():
        @pl.when(c == 0)
        def _():
            sf_ref[...] = jnp.zeros_like(sf_ref)

        for gi in range(SSD_GPS):
            _ssd_emit(c, **group(gi))


def _ssd_prepare(cc, x_ref, b_ref, dt_ref, par_ref, tri_ref, scratch, **unused):
    _, sb_ref, sball_ref, xt_ref, qn_ref, dtt_ref, qt_ref, tott_ref = scratch
    t = SSD_CHUNK
    rk = SSD_RANK
    prm = dtt_ref.shape[1]
    bm = b_ref[...]
    par = par_ref[...]
    lane = lax.broadcasted_iota(jnp.int32, (1, LANES), 1)
    dt = _softplus(dt_ref[...] + par[0:1, :])
    da = dt * (-jnp.exp(par[1:2, :]))
    d1 = da.astype(BF16)
    r1 = da - d1.astype(F32)
    d2 = r1.astype(BF16)
    d3 = (r1 - d2.astype(F32)).astype(BF16)
    cs3 = jnp.dot(tri_ref[...], jnp.concatenate([d1, d2, d3], axis=1), preferred_element_type=F32)
    cs = cs3[:, :LANES] + cs3[:, LANES:2 * LANES] + cs3[:, 2 * LANES:]
    q = (cs - jnp.where(lane >= rk, da, 0.0)) * LOG2E
    dt_t = dt.T[:prm]
    q_t = q.T[:prm]
    tot_t = jnp.broadcast_to((cs * LOG2E).T[:prm, t - 1:t], (prm, t))
    x_t = x_ref[...].astype(F32).T
    xt_ref[cc] = x_t
    qn_ref[cc] = q
    dtt_ref[cc] = dt_t
    qt_ref[cc] = q_t
    tott_ref[cc] = tot_t

    sb = sb_ref[...]
    sball_ref[cc] = sb.astype(BF16)
    wb = dt_t * jnp.exp2(q_t)
    xw = (x_t * _head_rows(wb, rk)).astype(BF16)
    upd = jnp.dot(xw, bm, preferred_element_type=F32)
    sb_ref[...] = _head_rows(jnp.exp2(tot_t[:, :SSD_STATE]), rk) * sb + upd


def _ssd_emit(c, x_ref, b_ref, c_ref, z_ref, dskip_ref, nw_ref, o_ref, scratch, **unused):
    sf_ref, _, sball_ref, xt_ref, qn_ref, dtt_ref, qt_ref, tott_ref = scratch
    t = SSD_CHUNK
    rk = SSD_RANK
    hd = SSD_HEAD_DIM
    bm = b_ref[...]
    x_t = xt_ref[c]
    q = qn_ref[c]
    dt_t = dtt_ref[c]
    q_t = qt_ref[c]
    tot_t = tott_ref[c]
    cm = c_ref[...]
    gt = lax.dot_general(bm, cm, (((1,), (1,)), ((), ())), preferred_element_type=F32)
    srow = lax.broadcasted_iota(jnp.int32, (t, t), 0)
    tcol = lax.broadcasted_iota(jnp.int32, (t, t), 1)
    causal = srow <= tcol
    anti = srow >= tcol
    ys = []
    for h in range(rk):
        hb = rk + h
        xh = x_t[h * hd:(h + 1) * hd, :]
        lhs = jnp.concatenate([(xh * dt_t[h:h + 1, :]).astype(BF16), (xh * dt_t[hb:hb + 1, :]).astype(BF16)],
                              axis=1)
        lf = jnp.where(causal, jnp.exp2(q_t[h:h + 1, :] - q[:, h:h + 1]), 0.0)
        lb = jnp.where(anti, jnp.exp2(q[:, hb:hb + 1] - q_t[hb:hb + 1, :]), 0.0)
        rhs = jnp.concatenate([(gt * lf).astype(BF16), (gt * lb).astype(BF16)], axis=0)
        ys.append(jnp.dot(lhs, rhs, preferred_element_type=F32))
    y_t = jnp.concatenate(ys, axis=0)

    sf = sf_ref[...]
    states = jnp.concatenate([sf.astype(BF16), sball_ref[c]], axis=0)
    off = lax.dot_general(states, cm, (((1,), (1,)), ((), ())), preferred_element_type=F32)
    y_t += off[:SSD_GW] * _head_rows(jnp.exp2(q_t), 0)
    y_t += off[SSD_GW:] * _head_rows(jnp.exp2(tot_t - q_t), rk)
    wf = dt_t * jnp.exp2(tot_t - q_t)
    xw = (x_t * _head_rows(wf, 0)).astype(BF16)
    upd = jnp.dot(xw, bm, preferred_element_type=F32)
    sf_ref[...] = _head_rows(jnp.exp2(tot_t[:, :SSD_STATE]), 0) * sf + upd

    y = y_t.T + dskip_ref[...] * x_ref[...].astype(F32)
    z = z_ref[...].astype(F32)
    y = y * (z * jax.nn.sigmoid(z))
    o_ref[...] = _rmsnorm(y, nw_ref[...]).astype(BF16)


def _ssd(xbc, zx, dt_all, par, dskip, nw, layer):
    seq = xbc.shape[0]
    nc = seq // SSD_CHUNK
    t = SSD_CHUNK
    b0 = SSD_WIDTH // SSD_STATE
    c0 = b0 + SSD_GROUPS
    tri = jnp.asarray(np.tril(np.ones((t, t), np.float32)), dtype=BF16)

    def cidx(p, c):
        return p * c + (1 - p) * (nc - 1 - c)

    gps = SSD_GPS
    assert b0 % gps == 0 and c0 % gps == 0
    return pl.pallas_call(
        _ssd_body,
        grid=(SSD_GROUPS // gps, 2, nc),
        in_specs=[
            pl.BlockSpec((t, gps * SSD_GW), lambda g, p, c: (cidx(p, c), g)),
            pl.BlockSpec((t, gps * SSD_STATE), lambda g, p, c: (cidx(p, c), b0 // gps + g)),
            pl.BlockSpec((t, gps * SSD_STATE), lambda g, p, c: (p * c, c0 // gps + g)),
            pl.BlockSpec((t, gps * SSD_GW), lambda g, p, c: (p * c, g)),
            pl.BlockSpec((t, gps * LANES), lambda g, p, c: ((1 - p) * (nc - 1 - c), g)),
            pl.BlockSpec((None, gps, SUBLANES, LANES), lambda g, p, c: (layer, g, 0, 0)),
            pl.BlockSpec((None, gps, 1, SSD_GW), lambda g, p, c: (layer, g, 0, 0)),
            pl.BlockSpec((None, gps, 1, SSD_GW), lambda g, p, c: (layer, g, 0, 0)),
            pl.BlockSpec((t, t), lambda g, p, c: (0, 0)),
        ],
        out_specs=pl.BlockSpec((t, gps * SSD_GW), lambda g, p, c: (p * c, g)),
        out_shape=jax.ShapeDtypeStruct((seq, SSD_WIDTH), BF16),
        scratch_shapes=[
            pltpu.VMEM((gps, SSD_GW, SSD_STATE), F32),
            pltpu.VMEM((gps, SSD_GW, SSD_STATE), F32),
            pltpu.VMEM((gps, nc, SSD_GW, SSD_STATE), BF16),
            pltpu.VMEM((gps, nc, SSD_GW, t), F32),
            pltpu.VMEM((gps, nc, t, LANES), F32),
            pltpu.VMEM((gps, nc, 2 * SUBLANES, t), F32),
            pltpu.VMEM((gps, nc, 2 * SUBLANES, t), F32),
            pltpu.VMEM((gps, nc, 2 * SUBLANES, t), F32),
        ],
        compiler_params=_cparams("arbitrary", "arbitrary", "arbitrary"),
        name="ssd_scan",
    )(xbc, xbc, xbc, zx, dt_all, par, dskip, nw, tri)


def _fw_body(cd_ref, sd_ref, w_ref, a_ref, b_ref):
    w = w_ref[...]
    a_ref[...] = jnp.dot(cd_ref[...], w, preferred_element_type=F32, precision=HIGHEST).astype(BF16)
    b_ref[...] = jnp.dot(sd_ref[...], w, preferred_element_type=F32, precision=HIGHEST).astype(BF16)


def _fourier_weights(fourier_w, layer):
    d = FOURIER_GD
    ang = 2.0 * np.pi * np.outer(np.arange(d), np.arange(d)) / d
    cd = jnp.asarray((np.cos(ang) / np.sqrt(d)).astype(np.float32))
    sd = jnp.asarray((np.sin(ang) / np.sqrt(d)).astype(np.float32))
    return pl.pallas_call(
        _fw_body,
        grid=(FOURIER_GROUPS,),
        in_specs=[
            pl.BlockSpec((d, d), lambda g: (0, 0)),
            pl.BlockSpec((d, d), lambda g: (0, 0)),
            pl.BlockSpec((None, None, d, d), lambda g: (layer, g, 0, 0)),
        ],
        out_specs=[pl.BlockSpec((None, d, d), lambda g: (g, 0, 0))] * 2,
        out_shape=[jax.ShapeDtypeStruct((FOURIER_GROUPS, d, d), BF16)] * 2,
        compiler_params=_cparams("parallel"),
        name="fourier_weights",
    )(cd, sd, fourier_w)


DFT_NB = 4


def _dft_a_body(x_ref, f_ref, tc_ref, ts_ref, o_ref):
    n1 = DFT_N1
    y = jnp.dot(f_ref[...], x_ref[...], preferred_element_type=F32)
    reps = FOURIER_WIDTH // LANES
    for b in range(tc_ref.shape[0]):
        sl = slice(b * FOURIER_WIDTH, (b + 1) * FOURIER_WIDTH)
        yr = y[:n1, sl]
        yi = y[n1:, sl]
        tc = jnp.tile(tc_ref[b], (1, reps))
        ts = jnp.tile(ts_ref[b], (1, reps))
        o_ref[:n1, sl] = (yr * tc + yi * ts).astype(BF16)
        o_ref[n1:, sl] = (yi * tc - yr * ts).astype(BF16)


def _dft_a(u):
    seq = u.shape[0]
    n1 = DFT_N1
    n2 = seq // n1
    nb = min(DFT_NB, n2)
    ang1 = 2.0 * np.pi * np.outer(np.arange(n1), np.arange(n1)) / n1
    f1 = jnp.asarray(np.concatenate([np.cos(ang1), -np.sin(ang1)], axis=0), dtype=BF16)
    angt = 2.0 * np.pi * np.outer(np.arange(n2), np.arange(n1)) / seq
    tc = jnp.asarray(np.repeat(np.cos(angt)[:, :, None], LANES, axis=2).astype(np.float32))
    ts = jnp.asarray(np.repeat(np.sin(angt)[:, :, None], LANES, axis=2).astype(np.float32))
    x2 = u.reshape(n1, n2 * FOURIER_WIDTH)
    return pl.pallas_call(
        _dft_a_body,
        grid=(n2 // nb,),
        in_specs=[
            pl.BlockSpec((n1, nb * FOURIER_WIDTH), lambda j: (0, j)),
            pl.BlockSpec((2 * n1, n1), lambda j: (0, 0)),
            pl.BlockSpec((nb, n1, LANES), lambda j: (j, 0, 0)),
            pl.BlockSpec((nb, n1, LANES), lambda j: (j, 0, 0)),
        ],
        out_specs=pl.BlockSpec((2 * n1, nb * FOURIER_WIDTH), lambda j: (0, j)),
        out_shape=jax.ShapeDtypeStruct((2 * n1, n2 * FOURIER_WIDTH), BF16),
        compiler_params=_cparams("parallel"),
        name="dft_stage_a",
    )(x2, f1, tc, ts)


def _dft_b_body(yr_ref, yi_ref, lr_ref, li_ref, a_ref, b_ref, o_ref):
    kb, n2, width = yr_ref.shape
    rhs = jnp.concatenate([yr_ref[...].reshape(kb * n2, width), yi_ref[...].reshape(kb * n2, width)], axis=0)
    zr = jnp.dot(lr_ref[...], rhs, preferred_element_type=F32).astype(BF16)
    zi = jnp.dot(li_ref[...], rhs, preferred_element_type=F32).astype(BF16)
    outs = []
    for g in range(FOURIER_GROUPS):
        sl = slice(g * FOURIER_GD, (g + 1) * FOURIER_GD)
        outs.append(jnp.dot(zr[:, sl], a_ref[g], preferred_element_type=F32)
                    + jnp.dot(zi[:, sl], b_ref[g], preferred_element_type=F32))
    o_ref[...] = jnp.concatenate(outs, axis=1).reshape(o_ref.shape)


def _dft_b(ya, fa, fb, seq):
    n1 = DFT_N1
    n2 = seq // n1
    kb = DFT_KB
    ang2 = 2.0 * np.pi * np.outer(np.arange(n2), np.arange(n2)) / n2
    c2 = np.cos(ang2) / np.sqrt(seq)
    s2 = np.sin(ang2) / np.sqrt(seq)
    eye = np.eye(kb)
    lr = np.concatenate([np.einsum('ab,kn->kabn', eye, c2).reshape(n2 * kb, kb * n2),
                         np.einsum('ab,kn->kabn', eye, s2).reshape(n2 * kb, kb * n2)], axis=1)
    li = np.concatenate([np.einsum('ab,kn->kabn', eye, -s2).reshape(n2 * kb, kb * n2),
                         np.einsum('ab,kn->kabn', eye, c2).reshape(n2 * kb, kb * n2)], axis=1)
    y3 = ya.reshape(2 * n1, n2, FOURIER_WIDTH)
    nk = n1 // kb
    out = pl.pallas_call(
        _dft_b_body,
        grid=(nk,),
        in_specs=[
            pl.BlockSpec((kb, n2, FOURIER_WIDTH), lambda j: (j, 0, 0)),
            pl.BlockSpec((kb, n2, FOURIER_WIDTH), lambda j: (nk + j, 0, 0)),
            pl.BlockSpec((n2 * kb, 2 * kb * n2), lambda j: (0, 0)),
            pl.BlockSpec((n2 * kb, 2 * kb * n2), lambda j: (0, 0)),
            pl.BlockSpec((FOURIER_GROUPS, FOURIER_GD, FOURIER_GD), lambda j: (0, 0, 0)),
            pl.BlockSpec((FOURIER_GROUPS, FOURIER_GD, FOURIER_GD), lambda j: (0, 0, 0)),
        ],
        out_specs=pl.BlockSpec((n2, kb, FOURIER_WIDTH), lambda j: (0, j, 0)),
        out_shape=jax.ShapeDtypeStruct((n2, n1, FOURIER_WIDTH), F32),
        compiler_params=_cparams("parallel"),
        name="dft_stage_b",
    )(y3, y3, jnp.asarray(lr, dtype=BF16), jnp.asarray(li, dtype=BF16), fa, fb)
    return out.reshape(seq, FOURIER_WIDTH)


def _outproj_body(*refs, n_lhs):
    x_ref = refs[0]
    lhs = refs[1:1 + n_lhs]
    ws = refs[1 + n_lhs:1 + 2 * n_lhs]
    o_ref = refs[1 + 2 * n_lhs]
    acc = x_ref[...]
    for a_ref, w_ref in zip(lhs, ws):
        acc = acc + jnp.dot(a_ref[...].astype(BF16), w_ref[...], preferred_element_type=F32)
    o_ref[...] = acc


def _outproj(x, lhs_list, w, layer):
    seq = x.shape[0]
    in_specs = [pl.BlockSpec((TM, TN), lambda j, i: (i, j))]
    for a in lhs_list:
        in_specs.append(pl.BlockSpec((TM, a.shape[1]), lambda j, i: (i, 0)))
    row = 0
    for a in lhs_list:
        k = a.shape[1]
        assert row % k == 0
        in_specs.append(pl.BlockSpec((None, k, TN), lambda j, i, rb=row // k: (layer, rb, j)))
        row += k
    return pl.pallas_call(
        functools.partial(_outproj_body, n_lhs=len(lhs_list)),
        grid=(D_MODEL // TN, seq // TM),
        in_specs=in_specs,
        out_specs=pl.BlockSpec((TM, TN), lambda j, i: (i, j)),
        out_shape=jax.ShapeDtypeStruct((seq, D_MODEL), F32),
        compiler_params=_cparams("parallel", "parallel"),
        name="outproj",
    )(x, *lhs_list, *([w] * len(lhs_list)))


SGU_V_TILES = SGU_WIDTH // TN
SGU_TILES = 2 * SGU_V_TILES


def _sgu_body(x_ref, nw_ref, w_ref, b_ref, vnw_ref, ws_ref, bs_ref, o_ref, h_ref, v_ref, ss_ref):
    j = pl.program_id(1)
    tm = x_ref.shape[0]

    @pl.when(j == 0)
    def _():
        h_ref[...] = _rmsnorm(x_ref[...], nw_ref[...]).astype(BF16)
        ss_ref[...] = jnp.zeros_like(ss_ref)

    acc = jnp.dot(h_ref[...], w_ref[...], preferred_element_type=F32) + b_ref[...]
    act = jax.nn.gelu(acc)

    @pl.when(j < SGU_V_TILES)
    def _():
        v_ref[j] = act.astype(BF16)
        ss_ref[...] += jnp.sum(act * act, axis=-1, keepdims=True)

    @pl.when(j == SGU_V_TILES - 1)
    def _():
        rs = lax.rsqrt(ss_ref[...] * (1.0 / SGU_WIDTH) + EPS)
        per_tile = TN // SGU_GD
        for g in range(SGU_GROUPS):
            tile, off = divmod(g, per_tile)
            sl = slice(off * SGU_GD, (off + 1) * SGU_GD)
            v = v_ref[tile, :, sl].astype(F32)
            v = (v * rs * vnw_ref[:, g * SGU_GD:(g + 1) * SGU_GD]).astype(BF16)
            bias = jnp.tile(bs_ref[g], (1, SGU_GD // LANES))
            for qc in range(tm // SGU_CHUNK):
                rows = slice(qc * SGU_CHUNK, (qc + 1) * SGU_CHUNK)
                mixed = jnp.dot(ws_ref[g], v[rows], preferred_element_type=F32) + bias
                v_ref[tile, rows, sl] = mixed.astype(BF16)

    @pl.when(j >= SGU_V_TILES)
    def _():
        o_ref[...] = (act * v_ref[j - SGU_V_TILES].astype(F32)).astype(BF16)


def _sgu(x, nw, w_uv, b_uv, vnw, w_s, b_s, layer, j_odd):
    seq = x.shape[0]
    tm = min(TMX, seq)
    wcol = lambda j: (j + SGU_V_TILES) % SGU_TILES
    return pl.pallas_call(
        _sgu_body,
        grid=(seq // tm, SGU_TILES),
        in_specs=[
            pl.BlockSpec((tm, D_MODEL), lambda i, j: (i, 0)),
            pl.BlockSpec((None, 1, D_MODEL), lambda i, j: (layer, 0, 0)),
            pl.BlockSpec((None, D_MODEL, TN), lambda i, j: (j_odd, 0, wcol(j))),
            pl.BlockSpec((None, 1, TN), lambda i, j: (j_odd, 0, wcol(j))),
            pl.BlockSpec((None, 1, SGU_WIDTH), lambda i, j: (j_odd, 0, 0)),
            pl.BlockSpec((None, SGU_GROUPS, SGU_CHUNK, SGU_CHUNK), lambda i, j: (j_odd, 0, 0, 0)),
            pl.BlockSpec((None, SGU_GROUPS, SGU_CHUNK, LANES), lambda i, j: (j_odd, 0, 0, 0)),
        ],
        out_specs=pl.BlockSpec((tm, TN), lambda i, j: (i, jnp.maximum(j - SGU_V_TILES, 0))),
        out_shape=jax.ShapeDtypeStruct((seq, SGU_WIDTH), BF16),
        scratch_shapes=[
            pltpu.VMEM((tm, D_MODEL), BF16),
            pltpu.VMEM((SGU_V_TILES, tm, TN), BF16),
            pltpu.VMEM((tm, 1), F32),
        ],
        compiler_params=_cparams("parallel", "arbitrary"),
        name="sgu",
    )(x, nw, w_uv, b_uv, vnw, w_s, b_s)


def _even_in_side_weights(w):
    n, d, _ = w.shape
    dt0 = ZX_WIDTH
    u0 = dt0 + 2 * SSD_HEADS
    w_dt = w[:, :, dt0:u0].reshape(n, d, 2, SSD_GROUPS, SSD_RANK)
    w_dt = jnp.transpose(w_dt, (0, 1, 3, 2, 4)).reshape(n, d, SSD_GROUPS, 2 * SSD_RANK)
    w_dt = jnp.pad(w_dt, ((0, 0), (0, 0), (0, 0), (0, LANES - 2 * SSD_RANK))).reshape(n, d, SSD_GROUPS * LANES)
    return w[:, :, u0:].astype(BF16), w_dt.astype(BF16)


def _group_lanes(p):
    n = p.shape[0]
    p = jnp.transpose(p.reshape(n, 2, SSD_GROUPS, SSD_RANK), (0, 2, 1, 3)).reshape(n, SSD_GROUPS, 2 * SSD_RANK)
    return jnp.pad(p, ((0, 0), (0, 0), (0, LANES - 2 * SSD_RANK)))


def kernel(x, ffn1_norm, ffn1_w_gate, ffn1_w_up, ffn1_w_down, mix_norm, ffn2_norm, ffn2_w_gate, ffn2_w_up,
           ffn2_w_down, even_w_in, ssd_conv_w, ssd_conv_b, ssd_dt_bias, ssd_a_log, ssd_d, ssd_norm, fourier_w,
           even_w_out, sgu_w_uv, sgu_b_uv, sgu_norm, sgu_w_s, sgu_b_s, odd_w_out, final_norm):
    bsz, seq, d = x.shape
    assert bsz == 1 and d == D_MODEL
    depth = ffn1_norm.shape[0]
    n_even = even_w_in.shape[0]
    xs = x.reshape(seq, d)

    row3 = lambda a: a.reshape(a.shape[0], 1, a.shape[1])
    assert depth >= 1
    ffn_f32 = {1: (ffn1_w_gate, ffn1_w_up, ffn1_w_down), 2: (ffn2_w_gate, ffn2_w_up, ffn2_w_down)}
    ffn_nw = {1: row3(ffn1_norm), 2: row3(ffn2_norm)}
    ffn_w = tuple(w[0].astype(BF16) for w in ffn_f32[1])

    def ffn(xs, which, i, ffn_w):
        last = which == 2 and i == depth - 1
        nxt = None if last else ((*ffn_f32[2], i) if which == 1 else (*ffn_f32[1], i + 1))
        return _ffn(xs, ffn_nw[which], ffn_w, i, next_w=nxt, out_norm=final_norm.reshape(1, d) if last else None)

    mixn = row3(mix_norm)
    w_in = even_w_in[:, :, :ZX_WIDTH].astype(BF16)
    w_in_u, w_in_dt = _even_in_side_weights(even_w_in)
    zeros = jnp.zeros((n_even, SSD_GROUPS, SUBLANES - 2, LANES), F32)
    par = jnp.concatenate([_group_lanes(ssd_dt_bias)[:, :, None, :], _group_lanes(ssd_a_log)[:, :, None, :], zeros],
                          axis=2)
    dskip = jnp.repeat(ssd_d, SSD_HEAD_DIM, axis=1).reshape(n_even, SSD_GROUPS, 1, SSD_GW)
    ssd_nw = ssd_norm.reshape(n_even, SSD_GROUPS, 1, SSD_GW)
    conv_b = row3(ssd_conv_b)
    w_out_even = even_w_out.astype(BF16)
    w_uv = sgu_w_uv.astype(BF16)
    b_uv = row3(sgu_b_uv)
    sgu_nw = row3(sgu_norm)
    w_s = sgu_w_s.astype(BF16)
    b_s = jnp.broadcast_to(sgu_b_s[..., None], sgu_b_s.shape + (LANES,))
    w_out_odd = odd_w_out.astype(BF16)

    for i in range(depth):
        xs, ffn_w = ffn(xs, 1, i, ffn_w)
        j = i // 2
        if i % 2 == 0:
            zx, u, dt_all = _inproj(xs, mixn, w_in, w_in_u, w_in_dt, i, j)
            xbc = _conv(zx, ssd_conv_w, conv_b, j)
            y_ssd = _ssd(xbc, zx, dt_all, par, dskip, ssd_nw, j)
            fa, fb = _fourier_weights(fourier_w, j)
            y_fft = _dft_b(_dft_a(u), fa, fb, seq)
            xs = _outproj(xs, [y_ssd, y_fft], w_out_even, j)
        else:
            gated = _sgu(xs, mixn, w_uv, b_uv, sgu_nw, w_s, b_s, i, j)
            xs = _outproj(xs, [gated], w_out_odd, j)
        xs, ffn_w = ffn(xs, 2, i, ffn_w)
    return xs.reshape(bsz, seq, d)
```

```python
import functools

import numpy as np
import jax
import jax.numpy as jnp
from jax import lax
from jax.experimental import pallas as pl
from jax.experimental.pallas import tpu as pltpu

F32 = jnp.float32
BF16 = jnp.bfloat16
HIGHEST = lax.Precision.HIGHEST
LOG2E = 1.4426950408889634

D_MODEL = 2048
D_FF = 5632
EPS = 1e-6
SSD_HEAD_DIM = 64
SSD_HEADS = 48
SSD_GROUPS = 8
SSD_RANK = SSD_HEADS // SSD_GROUPS
SSD_GW = SSD_RANK * SSD_HEAD_DIM
SSD_STATE = 128
SSD_CHUNK = 256
SSD_WIDTH = SSD_HEADS * SSD_HEAD_DIM
SSD_CONV = 5
SSD_CONV_CH = SSD_WIDTH + 2 * SSD_GROUPS * SSD_STATE
FOURIER_WIDTH = 1024
FOURIER_GROUPS = 4
FOURIER_GD = FOURIER_WIDTH // FOURIER_GROUPS
SGU_WIDTH = 4096
SGU_GROUPS = 8
SGU_GD = SGU_WIDTH // SGU_GROUPS
SGU_CHUNK = 128
ZX_WIDTH = SSD_WIDTH + SSD_CONV_CH

LANES = 128
SUBLANES = 8
VMEM_LIMIT = 56 * 1024 * 1024
VMEM_LIMIT_FFN = 60 * 1024 * 1024

TM = 512
TMX = 1024
TF = 512
TN = 1024
SSD_GPS = 2
DFT_N1 = 128
DFT_KB = 8


def _cparams(*sem, vmem_limit=VMEM_LIMIT):
    return pltpu.CompilerParams(dimension_semantics=sem, vmem_limit_bytes=vmem_limit)


def _rmsnorm(x, w):
    ms = jnp.mean(x * x, axis=-1, keepdims=True)
    return x * lax.rsqrt(ms + EPS) * w


def _ffn_body(*refs, cast_next, out_norm):
    x_ref, nw_ref, wg_ref, wu_ref, wd_ref = refs[:5]
    pos = 5
    nxt_in = refs[pos:pos + 3] if cast_next else ()
    pos += len(nxt_in)
    out_nw_ref = refs[pos] if out_norm else None
    pos += int(out_norm)
    o_ref = refs[pos]
    nxt_out = refs[pos + 1:pos + 1 + len(nxt_in)]
    xn_ref = refs[-1]
    j = pl.program_id(1)

    @pl.when(j == 0)
    def _():
        x = x_ref[...]
        xn_ref[...] = _rmsnorm(x, nw_ref[...]).astype(BF16)
        o_ref[...] = x

    xn = xn_ref[...]
    g = jnp.dot(xn, wg_ref[...], preferred_element_type=F32)
    u = jnp.dot(xn, wu_ref[...], preferred_element_type=F32)
    h = ((0.5 * g) * jax.nn.sigmoid(g) * u).astype(BF16)
    o_ref[...] += jnp.dot(h, wd_ref[...], preferred_element_type=F32)
    for src, dst in zip(nxt_in, nxt_out):
        dst[...] = src[...].astype(BF16)

    if out_norm:
        @pl.when(j == pl.num_programs(1) - 1)
        def _():
            o_ref[...] = _rmsnorm(o_ref[...], out_nw_ref[...])


def _ffn(x, nw, w, layer, next_w=None, out_norm=None):
    seq = x.shape[0]
    tm = min(TMX, seq)
    ni, nj = seq // tm, D_FF // TF
    in_specs = [
        pl.BlockSpec((tm, D_MODEL), lambda i, j: (i, 0)),
        pl.BlockSpec((None, 1, D_MODEL), lambda i, j: (layer, 0, 0)),
        pl.BlockSpec((D_MODEL, TF), lambda i, j: (0, j)),
        pl.BlockSpec((D_MODEL, TF), lambda i, j: (0, j)),
        pl.BlockSpec((TF, D_MODEL), lambda i, j: (j, 0)),
    ]
    operands = [x, nw, *w]
    out_specs = [pl.BlockSpec((tm, D_MODEL), lambda i, j: (i, 0))]
    out_shape = [jax.ShapeDtypeStruct((seq, D_MODEL), F32)]
    if next_w is not None:
        g32, u32, d32, nl = next_w
        assert D_MODEL % ni == 0
        rb = D_MODEL // ni
        in_specs += [pl.BlockSpec((None, rb, TF), lambda i, j: (nl, i, j)),
                     pl.BlockSpec((None, rb, TF), lambda i, j: (nl, i, j)),
                     pl.BlockSpec((None, TF, rb), lambda i, j: (nl, j, i))]
        operands += [g32, u32, d32]
        out_specs += [pl.BlockSpec((rb, TF), lambda i, j: (i, j)),
                      pl.BlockSpec((rb, TF), lambda i, j: (i, j)),
                      pl.BlockSpec((TF, rb), lambda i, j: (j, i))]
        out_shape += [jax.ShapeDtypeStruct((D_MODEL, D_FF), BF16),
                      jax.ShapeDtypeStruct((D_MODEL, D_FF), BF16),
                      jax.ShapeDtypeStruct((D_FF, D_MODEL), BF16)]
    if out_norm is not None:
        in_specs.append(pl.BlockSpec((1, D_MODEL), lambda i, j: (0, 0)))
        operands.append(out_norm)
    outs = pl.pallas_call(
        functools.partial(_ffn_body, cast_next=next_w is not None, out_norm=out_norm is not None),
        grid=(ni, nj),
        in_specs=in_specs,
        out_specs=out_specs,
        out_shape=out_shape,
        scratch_shapes=[pltpu.VMEM((tm, D_MODEL), BF16)],
        compiler_params=_cparams("parallel", "arbitrary", vmem_limit=VMEM_LIMIT_FFN),
        name="ffn",
    )(*operands)
    return outs[0], tuple(outs[1:])


N_ZX_TILES = ZX_WIDTH // TN
IN_TILES = N_ZX_TILES + 2


def _inproj_body(x_ref, nw_ref, w_ref, wu_ref, wdt_ref, cast_src, zx_ref, u_ref, dt_ref, cast_dst, h_ref):
    j = pl.program_id(1)
    cast_dst[...] = cast_src[...].astype(BF16)

    @pl.when(j == 0)
    def _():
        h_ref[...] = _rmsnorm(x_ref[...], nw_ref[...]).astype(BF16)

    @pl.when(j < N_ZX_TILES)
    def _():
        zx_ref[...] = lax.dot_general(h_ref[...], w_ref[...], (((1,), (1,)), ((), ())),
                                      preferred_element_type=F32).astype(BF16)

    @pl.when(j == N_ZX_TILES)
    def _():
        u_ref[...] = lax.dot_general(h_ref[...], wu_ref[...], (((1,), (1,)), ((), ())),
                                     preferred_element_type=F32).astype(BF16)

    @pl.when(j == N_ZX_TILES + 1)
    def _():
        dt_ref[...] = lax.dot_general(h_ref[...], wdt_ref[...], (((1,), (1,)), ((), ())),
                                      preferred_element_type=F32)


def _inproj(x, nw, w, w_u, w_dt, cast, layer, j_even):
    seq = x.shape[0]
    tm = min(TMX, seq)
    ni = seq // tm
    once = pl.Buffered(1)
    c_in, c_out, c_shape = _cast_specs(cast, ni, N_ZX_TILES, lambda i, j: (i, jnp.minimum(j, N_ZX_TILES - 1)))
    return pl.pallas_call(
        _inproj_body,
        grid=(ni, IN_TILES),
        in_specs=[
            pl.BlockSpec((tm, D_MODEL), lambda i, j: (i, 0), pipeline_mode=once),
            pl.BlockSpec((None, 1, D_MODEL), lambda i, j: (layer, 0, 0)),
            pl.BlockSpec((None, TN, D_MODEL), lambda i, j: (j_even, jnp.minimum(j, N_ZX_TILES - 1), 0)),
            pl.BlockSpec((None, FOURIER_WIDTH, D_MODEL), lambda i, j: (j_even, 0, 0), pipeline_mode=once),
            pl.BlockSpec((None, SSD_GROUPS * LANES, D_MODEL), lambda i, j: (j_even, 0, 0), pipeline_mode=once),
            c_in,
        ],
        out_specs=[
            pl.BlockSpec((tm, TN), lambda i, j: (i, jnp.minimum(j, N_ZX_TILES - 1))),
            pl.BlockSpec((tm, FOURIER_WIDTH), lambda i, j: (i, 0)),
            pl.BlockSpec((tm, SSD_GROUPS * LANES), lambda i, j: (i, 0)),
            c_out,
        ],
        out_shape=[
            jax.ShapeDtypeStruct((seq, ZX_WIDTH), BF16),
            jax.ShapeDtypeStruct((seq, FOURIER_WIDTH), BF16),
            jax.ShapeDtypeStruct((seq, SSD_GROUPS * LANES), F32),
            c_shape,
        ],
        scratch_shapes=[pltpu.VMEM((tm, D_MODEL), BF16)],
        compiler_params=_cparams("parallel", "arbitrary"),
        name="even_inproj",
    )(x, nw, w, w_u, w_dt, cast[0])


CONV_TR = 1024
CONV_TC = 512
CONV_HALO = 16


def _conv_body(xm_ref, xp_ref, xn_ref, w_ref, b_ref, o_ref):
    i = pl.program_id(0)
    last = pl.num_programs(0) - 1
    tr = xm_ref.shape[0]
    prev = xp_ref[...].astype(F32)[CONV_HALO - SUBLANES:]
    nxt = xn_ref[...].astype(F32)[:SUBLANES]
    ext = jnp.concatenate([jnp.where(i == 0, 0.0, prev), xm_ref[...].astype(F32), jnp.where(i == last, 0.0, nxt)],
                          axis=0)
    n = tr + 2 * SUBLANES
    w = w_ref[...]
    acc = jnp.broadcast_to(b_ref[...], o_ref.shape)
    half = SSD_CONV // 2
    for k in range(SSD_CONV):
        shifted = ext if k == half else pltpu.roll(ext, (half - k) % n, axis=0)
        acc = acc + shifted[SUBLANES:SUBLANES + tr, :] * w[k:k + 1, :]
    o_ref[...] = (acc * jax.nn.sigmoid(acc)).astype(BF16)


def _conv(zx, conv_w, conv_b, layer):
    seq = zx.shape[0]
    tr = min(CONV_TR, seq)
    col0 = SSD_WIDTH // CONV_TC
    hb = tr // CONV_HALO
    nhb = seq // CONV_HALO
    return pl.pallas_call(
        _conv_body,
        grid=(seq // tr, SSD_CONV_CH // CONV_TC),
        in_specs=[
            pl.BlockSpec((tr, CONV_TC), lambda i, j: (i, col0 + j)),
            pl.BlockSpec((CONV_HALO, CONV_TC), lambda i, j: (jnp.maximum(i * hb - 1, 0), col0 + j)),
            pl.BlockSpec((CONV_HALO, CONV_TC), lambda i, j: (jnp.minimum((i + 1) * hb, nhb - 1), col0 + j)),
            pl.BlockSpec((None, SSD_CONV, CONV_TC), lambda i, j: (layer, 0, j)),
            pl.BlockSpec((None, 1, CONV_TC), lambda i, j: (layer, 0, j)),
        ],
        out_specs=pl.BlockSpec((tr, CONV_TC), lambda i, j: (i, j)),
        out_shape=jax.ShapeDtypeStruct((seq, SSD_CONV_CH), BF16),
        compiler_params=_cparams("parallel", "parallel"),
        name="ssd_conv",
    )(zx, zx, zx, conv_w, conv_b)


def _softplus(v):
    return jnp.maximum(v, 0.0) + jnp.log1p(jnp.exp(-jnp.abs(v)))


def _head_rows(rows, first):
    n = rows.shape[1]
    return jnp.concatenate([jnp.broadcast_to(rows[first + h:first + h + 1, :], (SSD_HEAD_DIM, n))
                            for h in range(SSD_RANK)], axis=0)


def _ssd_body(x_ref, b_ref, c_ref, z_ref, dt_ref, par_ref, dskip_ref, nw_ref, tri_ref, o_ref, *scratch):
    phase = pl.program_id(1)
    c = pl.program_id(2)
    nc = pl.num_programs(2)
    sf_ref, sb_ref = scratch[:2]

    def group(gi):
        cols = lambda w: slice(gi * w, (gi + 1) * w)
        return dict(x_ref=x_ref.at[:, cols(SSD_GW)], b_ref=b_ref.at[:, cols(SSD_STATE)],
                    c_ref=c_ref.at[:, cols(SSD_STATE)], z_ref=z_ref.at[:, cols(SSD_GW)],
                    dt_ref=dt_ref.at[:, cols(LANES)], par_ref=par_ref.at[gi], dskip_ref=dskip_ref.at[gi],
                    nw_ref=nw_ref.at[gi], tri_ref=tri_ref, o_ref=o_ref.at[:, cols(SSD_GW)],
                    scratch=[s.at[gi] for s in scratch])

    @pl.when(phase == 0)
    def _():
        @pl.when(c == 0)
        def _():
            sb_ref[...] = jnp.zeros_like(sb_ref)

        for gi in range(SSD_GPS):
            _ssd_prepare(nc - 1 - c, **group(gi))

    @pl.when(phase == 1)
    def _():
        @pl.when(c == 0)
        def _():
            sf_ref[...] = jnp.zeros_like(sf_ref)

        for gi in range(SSD_GPS):
            _ssd_emit(c, **group(gi))


def _ssd_prepare(cc, x_ref, b_ref, dt_ref, par_ref, tri_ref, scratch, **unused):
    _, sb_ref, sball_ref, xt_ref, qn_ref, dtt_ref, qt_ref, tott_ref = scratch
    t = SSD_CHUNK
    rk = SSD_RANK
    prm = dtt_ref.shape[1]
    bm = b_ref[...]
    par = par_ref[...]
    lane = lax.broadcasted_iota(jnp.int32, (1, LANES), 1)
    dt = _softplus(dt_ref[...] + par[0:1, :])
    da = dt * (-jnp.exp(par[1:2, :]))
    d1 = da.astype(BF16)
    r1 = da - d1.astype(F32)
    d2 = r1.astype(BF16)
    d3 = (r1 - d2.astype(F32)).astype(BF16)
    cs3 = jnp.dot(tri_ref[...], jnp.concatenate([d1, d2, d3], axis=1), preferred_element_type=F32)
    cs = cs3[:, :LANES] + cs3[:, LANES:2 * LANES] + cs3[:, 2 * LANES:]
    q = (cs - jnp.where(lane >= rk, da, 0.0)) * LOG2E
    dt_t = dt.T[:prm]
    q_t = q.T[:prm]
    tot_t = jnp.broadcast_to((cs * LOG2E).T[:prm, t - 1:t], (prm, t))
    x_t = x_ref[...].astype(F32).T
    xt_ref[cc] = x_t
    qn_ref[cc] = q
    dtt_ref[cc] = dt_t
    qt_ref[cc] = q_t
    tott_ref[cc] = tot_t

    sb = sb_ref[...]
    sball_ref[cc] = sb.astype(BF16)
    wb = dt_t * jnp.exp2(q_t)
    xw = (x_t * _head_rows(wb, rk)).astype(BF16)
    upd = jnp.dot(xw, bm, preferred_element_type=F32)
    sb_ref[...] = _head_rows(jnp.exp2(tot_t[:, :SSD_STATE]), rk) * sb + upd


def _ssd_emit(c, x_ref, b_ref, c_ref, z_ref, dskip_ref, nw_ref, o_ref, scratch, **unused):
    sf_ref, _, sball_ref, xt_ref, qn_ref, dtt_ref, qt_ref, tott_ref = scratch
    t = SSD_CHUNK
    rk = SSD_RANK
    hd = SSD_HEAD_DIM
    bm = b_ref[...]
    x_t = xt_ref[c]
    q = qn_ref[c]
    dt_t = dtt_ref[c]
    q_t = qt_ref[c]
    tot_t = tott_ref[c]
    cm = c_ref[...]
    gt = lax.dot_general(bm, cm, (((1,), (1,)), ((), ())), preferred_element_type=F32)
    srow = lax.broadcasted_iota(jnp.int32, (t, t), 0)
    tcol = lax.broadcasted_iota(jnp.int32, (t, t), 1)
    causal = srow <= tcol
    anti = srow >= tcol
    ys = []
    for h in range(rk):
        hb = rk + h
        xh = x_t[h * hd:(h + 1) * hd, :]
        lhs = jnp.concatenate([(xh * dt_t[h:h + 1, :]).astype(BF16), (xh * dt_t[hb:hb + 1, :]).astype(BF16)],
                              axis=1)
        lf = jnp.where(causal, jnp.exp2(q_t[h:h + 1, :] - q[:, h:h + 1]), 0.0)
        lb = jnp.where(anti, jnp.exp2(q[:, hb:hb + 1] - q_t[hb:hb + 1, :]), 0.0)
        rhs = jnp.concatenate([(gt * lf).astype(BF16), (gt * lb).astype(BF16)], axis=0)
        ys.append(jnp.dot(lhs, rhs, preferred_element_type=F32))
    y_t = jnp.concatenate(ys, axis=0)

    sf = sf_ref[...]
    states = jnp.concatenate([sf.astype(BF16), sball_ref[c]], axis=0)
    off = lax.dot_general(states, cm, (((1,), (1,)), ((), ())), preferred_element_type=F32)
    y_t += off[:SSD_GW] * _head_rows(jnp.exp2(q_t), 0)
    y_t += off[SSD_GW:] * _head_rows(jnp.exp2(tot_t - q_t), rk)
    wf = dt_t * jnp.exp2(tot_t - q_t)
    xw = (x_t * _head_rows(wf, 0)).astype(BF16)
    upd = jnp.dot(xw, bm, preferred_element_type=F32)
    sf_ref[...] = _head_rows(jnp.exp2(tot_t[:, :SSD_STATE]), 0) * sf + upd

    y = y_t.T + dskip_ref[...] * x_ref[...].astype(F32)
    z = z_ref[...].astype(F32)
    y = y * (z * jax.nn.sigmoid(z))
    o_ref[...] = _rmsnorm(y, nw_ref[...]).astype(BF16)


def _ssd(xbc, zx, dt_all, par, dskip, nw, layer):
    seq = xbc.shape[0]
    nc = seq // SSD_CHUNK
    t = SSD_CHUNK
    b0 = SSD_WIDTH // SSD_STATE
    c0 = b0 + SSD_GROUPS
    tri = jnp.asarray(np.tril(np.ones((t, t), np.float32)), dtype=BF16)

    def cidx(p, c):
        return p * c + (1 - p) * (nc - 1 - c)

    gps = SSD_GPS
    assert b0 % gps == 0 and c0 % gps == 0
    return pl.pallas_call(
        _ssd_body,
        grid=(SSD_GROUPS // gps, 2, nc),
        in_specs=[
            pl.BlockSpec((t, gps * SSD_GW), lambda g, p, c: (cidx(p, c), g)),
            pl.BlockSpec((t, gps * SSD_STATE), lambda g, p, c: (cidx(p, c), b0 // gps + g)),
            pl.BlockSpec((t, gps * SSD_STATE), lambda g, p, c: (p * c, c0 // gps + g)),
            pl.BlockSpec((t, gps * SSD_GW), lambda g, p, c: (p * c, g)),
            pl.BlockSpec((t, gps * LANES), lambda g, p, c: ((1 - p) * (nc - 1 - c), g)),
            pl.BlockSpec((None, gps, SUBLANES, LANES), lambda g, p, c: (layer, g, 0, 0)),
            pl.BlockSpec((None, gps, 1, SSD_GW), lambda g, p, c: (layer, g, 0, 0)),
            pl.BlockSpec((None, gps, 1, SSD_GW), lambda g, p, c: (layer, g, 0, 0)),
            pl.BlockSpec((t, t), lambda g, p, c: (0, 0)),
        ],
        out_specs=pl.BlockSpec((t, gps * SSD_GW), lambda g, p, c: (p * c, g)),
        out_shape=jax.ShapeDtypeStruct((seq, SSD_WIDTH), BF16),
        scratch_shapes=[
            pltpu.VMEM((gps, SSD_GW, SSD_STATE), F32),
            pltpu.VMEM((gps, SSD_GW, SSD_STATE), F32),
            pltpu.VMEM((gps, nc, SSD_GW, SSD_STATE), BF16),
            pltpu.VMEM((gps, nc, SSD_GW, t), F32),
            pltpu.VMEM((gps, nc, t, LANES), F32),
            pltpu.VMEM((gps, nc, 2 * SUBLANES, t), F32),
            pltpu.VMEM((gps, nc, 2 * SUBLANES, t), F32),
            pltpu.VMEM((gps, nc, 2 * SUBLANES, t), F32),
        ],
        compiler_params=_cparams("arbitrary", "arbitrary", "arbitrary"),
        name="ssd_scan",
    )(xbc, xbc, xbc, zx, dt_all, par, dskip, nw, tri)


def _fw_body(cd_ref, sd_ref, w_ref, a_ref, b_ref):
    w = w_ref[...]
    a_ref[...] = jnp.dot(cd_ref[...], w, preferred_element_type=F32, precision=HIGHEST).astype(BF16)
    b_ref[...] = jnp.dot(sd_ref[...], w, preferred_element_type=F32, precision=HIGHEST).astype(BF16)


def _fourier_weights(fourier_w, layer):
    d = FOURIER_GD
    ang = 2.0 * np.pi * np.outer(np.arange(d), np.arange(d)) / d
    cd = jnp.asarray((np.cos(ang) / np.sqrt(d)).astype(np.float32))
    sd = jnp.asarray((np.sin(ang) / np.sqrt(d)).astype(np.float32))
    return pl.pallas_call(
        _fw_body,
        grid=(FOURIER_GROUPS,),
        in_specs=[
            pl.BlockSpec((d, d), lambda g: (0, 0)),
            pl.BlockSpec((d, d), lambda g: (0, 0)),
            pl.BlockSpec((None, None, d, d), lambda g: (layer, g, 0, 0)),
        ],
        out_specs=[pl.BlockSpec((None, d, d), lambda g: (g, 0, 0))] * 2,
        out_shape=[jax.ShapeDtypeStruct((FOURIER_GROUPS, d, d), BF16)] * 2,
        compiler_params=_cparams("parallel"),
        name="fourier_weights",
    )(cd, sd, fourier_w)


DFT_NB = 4


def _dft_a_body(x_ref, f_ref, tc_ref, ts_ref, o_ref):
    n1 = DFT_N1
    y = jnp.dot(f_ref[...], x_ref[...], preferred_element_type=F32)
    reps = FOURIER_WIDTH // LANES
    for b in range(tc_ref.shape[0]):
        sl = slice(b * FOURIER_WIDTH, (b + 1) * FOURIER_WIDTH)
        yr = y[:n1, sl]
        yi = y[n1:, sl]
        tc = jnp.tile(tc_ref[b], (1, reps))
        ts = jnp.tile(ts_ref[b], (1, reps))
        o_ref[:n1, sl] = (yr * tc + yi * ts).astype(BF16)
        o_ref[n1:, sl] = (yi * tc - yr * ts).astype(BF16)


def _dft_a(u):
    seq = u.shape[0]
    n1 = DFT_N1
    n2 = seq // n1
    nb = min(DFT_NB, n2)
    ang1 = 2.0 * np.pi * np.outer(np.arange(n1), np.arange(n1)) / n1
    f1 = jnp.asarray(np.concatenate([np.cos(ang1), -np.sin(ang1)], axis=0), dtype=BF16)
    angt = 2.0 * np.pi * np.outer(np.arange(n2), np.arange(n1)) / seq
    tc = jnp.asarray(np.repeat(np.cos(angt)[:, :, None], LANES, axis=2).astype(np.float32))
    ts = jnp.asarray(np.repeat(np.sin(angt)[:, :, None], LANES, axis=2).astype(np.float32))
    x2 = u.reshape(n1, n2 * FOURIER_WIDTH)
    return pl.pallas_call(
        _dft_a_body,
        grid=(n2 // nb,),
        in_specs=[
            pl.BlockSpec((n1, nb * FOURIER_WIDTH), lambda j: (0, j)),
            pl.BlockSpec((2 * n1, n1), lambda j: (0, 0)),
            pl.BlockSpec((nb, n1, LANES), lambda j: (j, 0, 0)),
            pl.BlockSpec((nb, n1, LANES), lambda j: (j, 0, 0)),
        ],
        out_specs=pl.BlockSpec((2 * n1, nb * FOURIER_WIDTH), lambda j: (0, j)),
        out_shape=jax.ShapeDtypeStruct((2 * n1, n2 * FOURIER_WIDTH), BF16),
        compiler_params=_cparams("parallel"),
        name="dft_stage_a",
    )(x2, f1, tc, ts)


def _dft_b_body(yr_ref, yi_ref, lr_ref, li_ref, a_ref, b_ref, o_ref):
    kb, n2, width = yr_ref.shape
    rhs = jnp.concatenate([yr_ref[...].reshape(kb * n2, width), yi_ref[...].reshape(kb * n2, width)], axis=0)
    zr = jnp.dot(lr_ref[...], rhs, preferred_element_type=F32).astype(BF16)
    zi = jnp.dot(li_ref[...], rhs, preferred_element_type=F32).astype(BF16)
    outs = []
    for g in range(FOURIER_GROUPS):
        sl = slice(g * FOURIER_GD, (g + 1) * FOURIER_GD)
        outs.append(jnp.dot(zr[:, sl], a_ref[g], preferred_element_type=F32)
                    + jnp.dot(zi[:, sl], b_ref[g], preferred_element_type=F32))
    o_ref[...] = jnp.concatenate(outs, axis=1).reshape(o_ref.shape)


def _dft_b(ya, fa, fb, seq):
    n1 = DFT_N1
    n2 = seq // n1
    kb = DFT_KB
    ang2 = 2.0 * np.pi * np.outer(np.arange(n2), np.arange(n2)) / n2
    c2 = np.cos(ang2) / np.sqrt(seq)
    s2 = np.sin(ang2) / np.sqrt(seq)
    eye = np.eye(kb)
    lr = np.concatenate([np.einsum('ab,kn->kabn', eye, c2).reshape(n2 * kb, kb * n2),
                         np.einsum('ab,kn->kabn', eye, s2).reshape(n2 * kb, kb * n2)], axis=1)
    li = np.concatenate([np.einsum('ab,kn->kabn', eye, -s2).reshape(n2 * kb, kb * n2),
                         np.einsum('ab,kn->kabn', eye, c2).reshape(n2 * kb, kb * n2)], axis=1)
    y3 = ya.reshape(2 * n1, n2, FOURIER_WIDTH)
    nk = n1 // kb
    out = pl.pallas_call(
        _dft_b_body,
        grid=(nk,),
        in_specs=[
            pl.BlockSpec((kb, n2, FOURIER_WIDTH), lambda j: (j, 0, 0)),
            pl.BlockSpec((kb, n2, FOURIER_WIDTH), lambda j: (nk + j, 0, 0)),
            pl.BlockSpec((n2 * kb, 2 * kb * n2), lambda j: (0, 0)),
            pl.BlockSpec((n2 * kb, 2 * kb * n2), lambda j: (0, 0)),
            pl.BlockSpec((FOURIER_GROUPS, FOURIER_GD, FOURIER_GD), lambda j: (0, 0, 0)),
            pl.BlockSpec((FOURIER_GROUPS, FOURIER_GD, FOURIER_GD), lambda j: (0, 0, 0)),
        ],
        out_specs=pl.BlockSpec((n2, kb, FOURIER_WIDTH), lambda j: (0, j, 0)),
        out_shape=jax.ShapeDtypeStruct((n2, n1, FOURIER_WIDTH), F32),
        compiler_params=_cparams("parallel"),
        name="dft_stage_b",
    )(y3, y3, jnp.asarray(lr, dtype=BF16), jnp.asarray(li, dtype=BF16), fa, fb)
    return out.reshape(seq, FOURIER_WIDTH)


def _cast_specs(cast, n_rows, n_cols, block_index):
    src, layer = cast
    _, r, c = src.shape
    assert r % n_rows == 0 and c % n_cols == 0
    blk = (r // n_rows, c // n_cols)
    in_spec = pl.BlockSpec((None,) + blk, lambda *g: (layer,) + tuple(block_index(*g)))
    out_spec = pl.BlockSpec(blk, lambda *g: tuple(block_index(*g)))
    return in_spec, out_spec, jax.ShapeDtypeStruct((r, c), BF16)


def _outproj_body(*refs, n_lhs, cast):
    x_ref = refs[0]
    lhs = refs[1:1 + n_lhs]
    ws = refs[1 + n_lhs:1 + 2 * n_lhs]
    o_ref = refs[1 + 2 * n_lhs + int(cast)]
    acc = x_ref[...]
    for a_ref, w_ref in zip(lhs, ws):
        acc = acc + jnp.dot(a_ref[...].astype(BF16), w_ref[...], preferred_element_type=F32)
    o_ref[...] = acc
    if cast:
        refs[-1][...] = refs[1 + 2 * n_lhs][...].astype(BF16)


def _outproj(x, lhs_list, w, cast=None):
    seq = x.shape[0]
    nj, ni = D_MODEL // TN, seq // TM
    in_specs = [pl.BlockSpec((TM, TN), lambda j, i: (i, j))]
    for a in lhs_list:
        in_specs.append(pl.BlockSpec((TM, a.shape[1]), lambda j, i: (i, 0)))
    row = 0
    for a in lhs_list:
        k = a.shape[1]
        assert row % k == 0
        in_specs.append(pl.BlockSpec((k, TN), lambda j, i, rb=row // k: (rb, j)))
        row += k
    operands = [x, *lhs_list, *([w] * len(lhs_list))]
    out_specs = [pl.BlockSpec((TM, TN), lambda j, i: (i, j))]
    out_shape = [jax.ShapeDtypeStruct((seq, D_MODEL), F32)]
    if cast is not None:
        c_in, c_out, c_shape = _cast_specs(cast, ni, nj, lambda j, i: (i, j))
        in_specs.append(c_in)
        operands.append(cast[0])
        out_specs.append(c_out)
        out_shape.append(c_shape)
    outs = pl.pallas_call(
        functools.partial(_outproj_body, n_lhs=len(lhs_list), cast=cast is not None),
        grid=(nj, ni),
        in_specs=in_specs,
        out_specs=out_specs,
        out_shape=out_shape,
        compiler_params=_cparams("parallel", "parallel"),
        name="outproj",
    )(*operands)
    return outs[0], (outs[1] if cast is not None else None)


SGU_V_TILES = SGU_WIDTH // TN
SGU_TILES = 2 * SGU_V_TILES


def _sgu_body(x_ref, nw_ref, w_ref, b_ref, vnw_ref, ws_ref, bs_ref, cast_src, o_ref, cast_dst, h_ref, v_ref, ss_ref):
    j = pl.program_id(1)
    tm = x_ref.shape[0]
    cast_dst[...] = cast_src[...].astype(BF16)

    @pl.when(j == 0)
    def _():
        h_ref[...] = _rmsnorm(x_ref[...], nw_ref[...]).astype(BF16)
        ss_ref[...] = jnp.zeros_like(ss_ref)

    acc = jnp.dot(h_ref[...], w_ref[...], preferred_element_type=F32) + b_ref[...]
    act = jax.nn.gelu(acc)

    @pl.when(j < SGU_V_TILES)
    def _():
        v_ref[j] = act.astype(BF16)
        ss_ref[...] += jnp.sum(act * act, axis=-1, keepdims=True)

    @pl.when(j == SGU_V_TILES - 1)
    def _():
        rs = lax.rsqrt(ss_ref[...] * (1.0 / SGU_WIDTH) + EPS)
        per_tile = TN // SGU_GD
        for g in range(SGU_GROUPS):
            tile, off = divmod(g, per_tile)
            sl = slice(off * SGU_GD, (off + 1) * SGU_GD)
            v = v_ref[tile, :, sl].astype(F32)
            v = (v * rs * vnw_ref[:, g * SGU_GD:(g + 1) * SGU_GD]).astype(BF16)
            bias = jnp.tile(bs_ref[g], (1, SGU_GD // LANES))
            for qc in range(tm // SGU_CHUNK):
                rows = slice(qc * SGU_CHUNK, (qc + 1) * SGU_CHUNK)
                mixed = jnp.dot(ws_ref[g], v[rows], preferred_element_type=F32) + bias
                v_ref[tile, rows, sl] = mixed.astype(BF16)

    @pl.when(j >= SGU_V_TILES)
    def _():
        o_ref[...] = (act * v_ref[j - SGU_V_TILES].astype(F32)).astype(BF16)


def _sgu(x, nw, w_uv, b_uv, vnw, w_s, b_s, cast, layer, j_odd):
    seq = x.shape[0]
    tm = min(TMX, seq)
    ni = seq // tm
    wcol = lambda j: (j + SGU_V_TILES) % SGU_TILES
    c_in, c_out, c_shape = _cast_specs(cast, ni, SGU_TILES, lambda i, j: (i, j))
    return pl.pallas_call(
        _sgu_body,
        grid=(ni, SGU_TILES),
        in_specs=[
            pl.BlockSpec((tm, D_MODEL), lambda i, j: (i, 0)),
            pl.BlockSpec((None, 1, D_MODEL), lambda i, j: (layer, 0, 0)),
            pl.BlockSpec((D_MODEL, TN), lambda i, j: (0, wcol(j))),
            pl.BlockSpec((None, 1, TN), lambda i, j: (j_odd, 0, wcol(j))),
            pl.BlockSpec((None, 1, SGU_WIDTH), lambda i, j: (j_odd, 0, 0)),
            pl.BlockSpec((None, SGU_GROUPS, SGU_CHUNK, SGU_CHUNK), lambda i, j: (j_odd, 0, 0, 0)),
            pl.BlockSpec((None, SGU_GROUPS, SGU_CHUNK, LANES), lambda i, j: (j_odd, 0, 0, 0)),
            c_in,
        ],
        out_specs=[pl.BlockSpec((tm, TN), lambda i, j: (i, jnp.maximum(j - SGU_V_TILES, 0))), c_out],
        out_shape=[jax.ShapeDtypeStruct((seq, SGU_WIDTH), BF16), c_shape],
        scratch_shapes=[
            pltpu.VMEM((tm, D_MODEL), BF16),
            pltpu.VMEM((SGU_V_TILES, tm, TN), BF16),
            pltpu.VMEM((tm, 1), F32),
        ],
        compiler_params=_cparams("parallel", "arbitrary"),
        name="sgu",
    )(x, nw, w_uv, b_uv, vnw, w_s, b_s, cast[0])


def _even_in_weights(w):
    n, d = w.shape[0], w.shape[1]
    wt = jnp.swapaxes(w, 1, 2).astype(BF16)
    dt0 = ZX_WIDTH
    u0 = dt0 + 2 * SSD_HEADS
    w_dt = wt[:, dt0:u0, :].reshape(n, 2, SSD_GROUPS, SSD_RANK, d)
    w_dt = jnp.transpose(w_dt, (0, 2, 1, 3, 4)).reshape(n, SSD_GROUPS, 2 * SSD_RANK, d)
    w_dt = jnp.pad(w_dt, ((0, 0), (0, 0), (0, LANES - 2 * SSD_RANK), (0, 0))).reshape(n, SSD_GROUPS * LANES, d)
    return wt, wt[:, u0:, :], w_dt


def _group_lanes(p):
    n = p.shape[0]
    p = jnp.transpose(p.reshape(n, 2, SSD_GROUPS, SSD_RANK), (0, 2, 1, 3)).reshape(n, SSD_GROUPS, 2 * SSD_RANK)
    return jnp.pad(p, ((0, 0), (0, 0), (0, LANES - 2 * SSD_RANK)))


def kernel(x, ffn1_norm, ffn1_w_gate, ffn1_w_up, ffn1_w_down, mix_norm, ffn2_norm, ffn2_w_gate, ffn2_w_up,
           ffn2_w_down, even_w_in, ssd_conv_w, ssd_conv_b, ssd_dt_bias, ssd_a_log, ssd_d, ssd_norm, fourier_w,
           even_w_out, sgu_w_uv, sgu_b_uv, sgu_norm, sgu_w_s, sgu_b_s, odd_w_out, final_norm):
    bsz, seq, d = x.shape
    assert bsz == 1 and d == D_MODEL
    depth = ffn1_norm.shape[0]
    n_even = even_w_in.shape[0]
    xs = x.reshape(seq, d)

    row3 = lambda a: a.reshape(a.shape[0], 1, a.shape[1])
    assert depth >= 1
    ffn_f32 = {1: (ffn1_w_gate, ffn1_w_up, ffn1_w_down), 2: (ffn2_w_gate, ffn2_w_up, ffn2_w_down)}
    ffn_nw = {1: row3(ffn1_norm), 2: row3(ffn2_norm)}
    ffn_w = tuple(w[0].astype(BF16) for w in ffn_f32[1])

    def ffn(xs, which, i, ffn_w):
        last = which == 2 and i == depth - 1
        nxt = None if last else ((*ffn_f32[2], i) if which == 1 else (*ffn_f32[1], i + 1))
        return _ffn(xs, ffn_nw[which], ffn_w, i, next_w=nxt, out_norm=final_norm.reshape(1, d) if last else None)

    mixn = row3(mix_norm)
    w_in, w_in_u, w_in_dt = _even_in_weights(even_w_in)
    zeros = jnp.zeros((n_even, SSD_GROUPS, SUBLANES - 2, LANES), F32)
    par = jnp.concatenate([_group_lanes(ssd_dt_bias)[:, :, None, :], _group_lanes(ssd_a_log)[:, :, None, :], zeros],
                          axis=2)
    dskip = jnp.repeat(ssd_d, SSD_HEAD_DIM, axis=1).reshape(n_even, SSD_GROUPS, 1, SSD_GW)
    ssd_nw = ssd_norm.reshape(n_even, SSD_GROUPS, 1, SSD_GW)
    conv_b = row3(ssd_conv_b)
    b_uv = row3(sgu_b_uv)
    sgu_nw = row3(sgu_norm)
    w_s = sgu_w_s.astype(BF16)
    b_s = jnp.broadcast_to(sgu_b_s[..., None], sgu_b_s.shape + (LANES,))

    w_uv = None
    for i in range(depth):
        xs, ffn_w = ffn(xs, 1, i, ffn_w)
        j = i // 2
        if i % 2 == 0:
            zx, u, dt_all, w_out = _inproj(xs, mixn, w_in, w_in_u, w_in_dt, (even_w_out, j), i, j)
            xbc = _conv(zx, ssd_conv_w, conv_b, j)
            y_ssd = _ssd(xbc, zx, dt_all, par, dskip, ssd_nw, j)
            fa, fb = _fourier_weights(fourier_w, j)
            y_fft = _dft_b(_dft_a(u), fa, fb, seq)
            nxt = (sgu_w_uv, j) if i + 1 < depth else None
            xs, w_uv = _outproj(xs, [y_ssd, y_fft], w_out, cast=nxt)
        else:
            gated, w_out = _sgu(xs, mixn, w_uv, b_uv, sgu_nw, w_s, b_s, (odd_w_out, j), i, j)
            xs, _ = _outproj(xs, [gated], w_out)
        xs, ffn_w = ffn(xs, 2, i, ffn_w)
    return xs.reshape(bsz, seq, d)
```

```python
import functools

import numpy as np
import jax
import jax.numpy as jnp
from jax import lax
from jax.experimental import pallas as pl
from jax.experimental.pallas import tpu as pltpu

F32 = jnp.float32
BF16 = jnp.bfloat16
HIGHEST = lax.Precision.HIGHEST
LOG2E = 1.4426950408889634

D_MODEL = 2048
D_FF = 5632
EPS = 1e-6
SSD_HEAD_DIM = 64
SSD_HEADS = 48
SSD_GROUPS = 8
SSD_RANK = SSD_HEADS // SSD_GROUPS
SSD_GW = SSD_RANK * SSD_HEAD_DIM
SSD_STATE = 128
SSD_CHUNK = 256
SSD_WIDTH = SSD_HEADS * SSD_HEAD_DIM
SSD_CONV = 5
SSD_CONV_CH = SSD_WIDTH + 2 * SSD_GROUPS * SSD_STATE
FOURIER_WIDTH = 1024
FOURIER_GROUPS = 4
FOURIER_GD = FOURIER_WIDTH // FOURIER_GROUPS
SGU_WIDTH = 4096
SGU_GROUPS = 8
SGU_GD = SGU_WIDTH // SGU_GROUPS
SGU_CHUNK = 128
ZX_WIDTH = SSD_WIDTH + SSD_CONV_CH

LANES = 128
SUBLANES = 8
VMEM_LIMIT = 56 * 1024 * 1024
VMEM_LIMIT_BIG = 60 * 1024 * 1024

TM = 512
TMX = 1024
TF = 512
TN = 1024
SSD_GPS = 2
DFT_N1 = 128
DFT_KB = 8


def _cparams(*sem, vmem_limit=VMEM_LIMIT):
    return pltpu.CompilerParams(dimension_semantics=sem, vmem_limit_bytes=vmem_limit)


def _rmsnorm(x, w):
    ms = jnp.mean(x * x, axis=-1, keepdims=True)
    return x * lax.rsqrt(ms + EPS) * w


def _ffn_body(*refs, cast_next, out_norm):
    x_ref, nw_ref, wg_ref, wu_ref, wd_ref = refs[:5]
    pos = 5
    nxt_in = refs[pos:pos + 3] if cast_next else ()
    pos += len(nxt_in)
    out_nw_ref = refs[pos] if out_norm else None
    pos += int(out_norm)
    o_ref = refs[pos]
    nxt_out = refs[pos + 1:pos + 1 + len(nxt_in)]
    xn_ref = refs[-1]
    j = pl.program_id(1)

    @pl.when(j == 0)
    def _():
        x = x_ref[...]
        xn_ref[...] = _rmsnorm(x, nw_ref[...]).astype(BF16)
        o_ref[...] = x

    xn = xn_ref[...]
    g = jnp.dot(xn, wg_ref[...], preferred_element_type=F32)
    u = jnp.dot(xn, wu_ref[...], preferred_element_type=F32)
    h = ((0.5 * g) * jax.nn.sigmoid(g) * u).astype(BF16)
    o_ref[...] += jnp.dot(h, wd_ref[...], preferred_element_type=F32)
    for src, dst in zip(nxt_in, nxt_out):
        dst[...] = src[...].astype(BF16)

    if out_norm:
        @pl.when(j == pl.num_programs(1) - 1)
        def _():
            o_ref[...] = _rmsnorm(o_ref[...], out_nw_ref[...])


def _ffn(x, nw, w, layer, next_w=None, out_norm=None):
    seq = x.shape[0]
    tm = min(TMX, seq)
    ni, nj = seq // tm, D_FF // TF
    in_specs = [
        pl.BlockSpec((tm, D_MODEL), lambda i, j: (i, 0)),
        pl.BlockSpec((None, 1, D_MODEL), lambda i, j: (layer, 0, 0)),
        pl.BlockSpec((D_MODEL, TF), lambda i, j: (0, j)),
        pl.BlockSpec((D_MODEL, TF), lambda i, j: (0, j)),
        pl.BlockSpec((TF, D_MODEL), lambda i, j: (j, 0)),
    ]
    operands = [x, nw, *w]
    out_specs = [pl.BlockSpec((tm, D_MODEL), lambda i, j: (i, 0))]
    out_shape = [jax.ShapeDtypeStruct((seq, D_MODEL), F32)]
    if next_w is not None:
        g32, u32, d32, nl = next_w
        assert D_MODEL % ni == 0
        rb = D_MODEL // ni
        in_specs += [pl.BlockSpec((None, rb, TF), lambda i, j: (nl, i, j)),
                     pl.BlockSpec((None, rb, TF), lambda i, j: (nl, i, j)),
                     pl.BlockSpec((None, TF, rb), lambda i, j: (nl, j, i))]
        operands += [g32, u32, d32]
        out_specs += [pl.BlockSpec((rb, TF), lambda i, j: (i, j)),
                      pl.BlockSpec((rb, TF), lambda i, j: (i, j)),
                      pl.BlockSpec((TF, rb), lambda i, j: (j, i))]
        out_shape += [jax.ShapeDtypeStruct((D_MODEL, D_FF), BF16),
                      jax.ShapeDtypeStruct((D_MODEL, D_FF), BF16),
                      jax.ShapeDtypeStruct((D_FF, D_MODEL), BF16)]
    if out_norm is not None:
        in_specs.append(pl.BlockSpec((1, D_MODEL), lambda i, j: (0, 0)))
        operands.append(out_norm)
    outs = pl.pallas_call(
        functools.partial(_ffn_body, cast_next=next_w is not None, out_norm=out_norm is not None),
        grid=(ni, nj),
        in_specs=in_specs,
        out_specs=out_specs,
        out_shape=out_shape,
        scratch_shapes=[pltpu.VMEM((tm, D_MODEL), BF16)],
        compiler_params=_cparams("parallel", "arbitrary", vmem_limit=VMEM_LIMIT_BIG),
        name="ffn",
    )(*operands)
    return outs[0], tuple(outs[1:])


N_ZX_TILES = ZX_WIDTH // TN
IN_TILES = N_ZX_TILES + 2


def _inproj_body(x_ref, nw_ref, w_ref, wu_ref, wdt_ref, cast_src, zx_ref, u_ref, dt_ref, cast_dst, h_ref):
    j = pl.program_id(1)
    cast_dst[...] = cast_src[...].astype(BF16)

    @pl.when(j == 0)
    def _():
        h_ref[...] = _rmsnorm(x_ref[...], nw_ref[...]).astype(BF16)

    @pl.when(j < N_ZX_TILES)
    def _():
        zx_ref[...] = lax.dot_general(h_ref[...], w_ref[...], (((1,), (1,)), ((), ())),
                                      preferred_element_type=F32).astype(BF16)

    @pl.when(j == N_ZX_TILES)
    def _():
        u_ref[...] = lax.dot_general(h_ref[...], wu_ref[...], (((1,), (1,)), ((), ())),
                                     preferred_element_type=F32).astype(BF16)

    @pl.when(j == N_ZX_TILES + 1)
    def _():
        dt_ref[...] = lax.dot_general(h_ref[...], wdt_ref[...], (((1,), (1,)), ((), ())),
                                      preferred_element_type=F32)


def _inproj(x, nw, w, w_u, w_dt, cast, layer, j_even):
    seq = x.shape[0]
    tm = min(TMX, seq)
    ni = seq // tm
    once = pl.Buffered(1)
    c_in, c_out, c_shape = _cast_specs(cast, ni, N_ZX_TILES, lambda i, j: (i, jnp.minimum(j, N_ZX_TILES - 1)))
    return pl.pallas_call(
        _inproj_body,
        grid=(ni, IN_TILES),
        in_specs=[
            pl.BlockSpec((tm, D_MODEL), lambda i, j: (i, 0)),
            pl.BlockSpec((None, 1, D_MODEL), lambda i, j: (layer, 0, 0)),
            pl.BlockSpec((None, TN, D_MODEL), lambda i, j: (j_even, jnp.minimum(j, N_ZX_TILES - 1), 0)),
            pl.BlockSpec((None, FOURIER_WIDTH, D_MODEL), lambda i, j: (j_even, 0, 0), pipeline_mode=once),
            pl.BlockSpec((None, SSD_GROUPS * LANES, D_MODEL), lambda i, j: (j_even, 0, 0), pipeline_mode=once),
            c_in,
        ],
        out_specs=[
            pl.BlockSpec((tm, TN), lambda i, j: (i, jnp.minimum(j, N_ZX_TILES - 1))),
            pl.BlockSpec((tm, FOURIER_WIDTH), lambda i, j: (i, 0)),
            pl.BlockSpec((tm, SSD_GROUPS * LANES), lambda i, j: (i, 0)),
            c_out,
        ],
        out_shape=[
            jax.ShapeDtypeStruct((seq, ZX_WIDTH), BF16),
            jax.ShapeDtypeStruct((seq, FOURIER_WIDTH), BF16),
            jax.ShapeDtypeStruct((seq, SSD_GROUPS * LANES), F32),
            c_shape,
        ],
        scratch_shapes=[pltpu.VMEM((tm, D_MODEL), BF16)],
        compiler_params=_cparams("parallel", "arbitrary", vmem_limit=VMEM_LIMIT_BIG),
        name="even_inproj",
    )(x, nw, w, w_u, w_dt, cast[0])


CONV_TR = 1024
CONV_TC = 512
CONV_HALO = 16


def _conv_body(xm_ref, xp_ref, xn_ref, w_ref, b_ref, o_ref):
    i = pl.program_id(0)
    last = pl.num_programs(0) - 1
    tr = xm_ref.shape[0]
    prev = xp_ref[...].astype(F32)[CONV_HALO - SUBLANES:]
    nxt = xn_ref[...].astype(F32)[:SUBLANES]
    ext = jnp.concatenate([jnp.where(i == 0, 0.0, prev), xm_ref[...].astype(F32), jnp.where(i == last, 0.0, nxt)],
                          axis=0)
    n = tr + 2 * SUBLANES
    w = w_ref[...]
    acc = jnp.broadcast_to(b_ref[...], o_ref.shape)
    half = SSD_CONV // 2
    for k in range(SSD_CONV):
        shifted = ext if k == half else pltpu.roll(ext, (half - k) % n, axis=0)
        acc = acc + shifted[SUBLANES:SUBLANES + tr, :] * w[k:k + 1, :]
    o_ref[...] = (acc * jax.nn.sigmoid(acc)).astype(BF16)


def _conv(zx, conv_w, conv_b, layer):
    seq = zx.shape[0]
    tr = min(CONV_TR, seq)
    col0 = SSD_WIDTH // CONV_TC
    hb = tr // CONV_HALO
    nhb = seq // CONV_HALO
    return pl.pallas_call(
        _conv_body,
        grid=(seq // tr, SSD_CONV_CH // CONV_TC),
        in_specs=[
            pl.BlockSpec((tr, CONV_TC), lambda i, j: (i, col0 + j)),
            pl.BlockSpec((CONV_HALO, CONV_TC), lambda i, j: (jnp.maximum(i * hb - 1, 0), col0 + j)),
            pl.BlockSpec((CONV_HALO, CONV_TC), lambda i, j: (jnp.minimum((i + 1) * hb, nhb - 1), col0 + j)),
            pl.BlockSpec((None, SSD_CONV, CONV_TC), lambda i, j: (layer, 0, j)),
            pl.BlockSpec((None, 1, CONV_TC), lambda i, j: (layer, 0, j)),
        ],
        out_specs=pl.BlockSpec((tr, CONV_TC), lambda i, j: (i, j)),
        out_shape=jax.ShapeDtypeStruct((seq, SSD_CONV_CH), BF16),
        compiler_params=_cparams("parallel", "parallel"),
        name="ssd_conv",
    )(zx, zx, zx, conv_w, conv_b)


def _softplus(v):
    return jnp.maximum(v, 0.0) + jnp.log1p(jnp.exp(-jnp.abs(v)))


def _head_rows(rows, first):
    n = rows.shape[1]
    return jnp.concatenate([jnp.broadcast_to(rows[first + h:first + h + 1, :], (SSD_HEAD_DIM, n))
                            for h in range(SSD_RANK)], axis=0)


def _ssd_body(x_ref, b_ref, c_ref, z_ref, dt_ref, par_ref, dskip_ref, nw_ref, tri_ref, o_ref, *scratch):
    phase = pl.program_id(1)
    c = pl.program_id(2)
    nc = pl.num_programs(2)
    sf_ref, sb_ref = scratch[:2]

    def group(gi):
        cols = lambda w: slice(gi * w, (gi + 1) * w)
        return dict(x_ref=x_ref.at[:, cols(SSD_GW)], b_ref=b_ref.at[:, cols(SSD_STATE)],
                    c_ref=c_ref.at[:, cols(SSD_STATE)], z_ref=z_ref.at[:, cols(SSD_GW)],
                    dt_ref=dt_ref.at[:, cols(LANES)], par_ref=par_ref.at[gi], dskip_ref=dskip_ref.at[gi],
                    nw_ref=nw_ref.at[gi], tri_ref=tri_ref, o_ref=o_ref.at[:, cols(SSD_GW)],
                    scratch=[s.at[gi] for s in scratch])

    @pl.when(phase == 0)
    def _():
        @pl.when(c == 0)
        def _():
            sb_ref[...] = jnp.zeros_like(sb_ref)

        for gi in range(SSD_GPS):
            _ssd_prepare(nc - 1 - c, **group(gi))

    @pl.when(phase == 1)
    def _():
        @pl.when(c == 0)
        def _():
            sf_ref[...] = jnp.zeros_like(sf_ref)

        for gi in range(SSD_GPS):
            _ssd_emit(c, **group(gi))


def _ssd_prepare(cc, x_ref, b_ref, dt_ref, par_ref, tri_ref, scratch, **unused):
    _, sb_ref, sball_ref, xt_ref, qn_ref, dtt_ref, qt_ref, tott_ref = scratch
    t = SSD_CHUNK
    rk = SSD_RANK
    prm = dtt_ref.shape[1]
    bm = b_ref[...]
    par = par_ref[...]
    lane = lax.broadcasted_iota(jnp.int32, (1, LANES), 1)
    dt = _softplus(dt_ref[...] + par[0:1, :])
    da = dt * (-jnp.exp(par[1:2, :]))
    d1 = da.astype(BF16)
    r1 = da - d1.astype(F32)
    d2 = r1.astype(BF16)
    d3 = (r1 - d2.astype(F32)).astype(BF16)
    cs3 = jnp.dot(tri_ref[...], jnp.concatenate([d1, d2, d3], axis=1), preferred_element_type=F32)
    cs = cs3[:, :LANES] + cs3[:, LANES:2 * LANES] + cs3[:, 2 * LANES:]
    q = (cs - jnp.where(lane >= rk, da, 0.0)) * LOG2E
    dt_t = dt.T[:prm]
    q_t = q.T[:prm]
    tot_t = jnp.broadcast_to((cs * LOG2E).T[:prm, t - 1:t], (prm, t))
    x_t = x_ref[...].astype(F32).T
    xt_ref[cc] = x_t
    qn_ref[cc] = q
    dtt_ref[cc] = dt_t
    qt_ref[cc] = q_t
    tott_ref[cc] = tot_t

    sb = sb_ref[...]
    sball_ref[cc] = sb.astype(BF16)
    wb = dt_t * jnp.exp2(q_t)
    xw = (x_t * _head_rows(wb, rk)).astype(BF16)
    upd = jnp.dot(xw, bm, preferred_element_type=F32)
    sb_ref[...] = _head_rows(jnp.exp2(tot_t[:, :SSD_STATE]), rk) * sb + upd


def _ssd_emit(c, x_ref, b_ref, c_ref, z_ref, dskip_ref, nw_ref, o_ref, scratch, **unused):
    sf_ref, _, sball_ref, xt_ref, qn_ref, dtt_ref, qt_ref, tott_ref = scratch
    t = SSD_CHUNK
    rk = SSD_RANK
    hd = SSD_HEAD_DIM
    bm = b_ref[...]
    x_t = xt_ref[c]
    q = qn_ref[c]
    dt_t = dtt_ref[c]
    q_t = qt_ref[c]
    tot_t = tott_ref[c]
    cm = c_ref[...]
    gt = lax.dot_general(bm, cm, (((1,), (1,)), ((), ())), preferred_element_type=F32)
    hf = t // 2
    srow = lax.broadcasted_iota(jnp.int32, (hf, hf), 0)
    tcol = lax.broadcasted_iota(jnp.int32, (hf, hf), 1)
    causal = srow <= tcol
    anti = srow >= tcol
    zero = jnp.zeros((hf, hf), BF16)
    g00, g01, g10, g11 = gt[:hf, :hf], gt[:hf, hf:], gt[hf:, :hf], gt[hf:, hf:]
    ys = []
    for h in range(rk):
        hb = rk + h
        xh = x_t[h * hd:(h + 1) * hd, :]
        lhs = jnp.concatenate([(xh * dt_t[h:h + 1, :]).astype(BF16), (xh * dt_t[hb:hb + 1, :]).astype(BF16)],
                              axis=1)
        qs, qt = q[:, h:h + 1], q_t[h:h + 1, :]
        f00 = (g00 * jnp.where(causal, jnp.exp2(qt[:, :hf] - qs[:hf]), 0.0)).astype(BF16)
        f01 = (g01 * jnp.exp2(qt[:, hf:] - qs[:hf])).astype(BF16)
        f11 = (g11 * jnp.where(causal, jnp.exp2(qt[:, hf:] - qs[hf:]), 0.0)).astype(BF16)
        qs, qt = q[:, hb:hb + 1], q_t[hb:hb + 1, :]
        b00 = (g00 * jnp.where(anti, jnp.exp2(qs[:hf] - qt[:, :hf]), 0.0)).astype(BF16)
        b10 = (g10 * jnp.exp2(qs[hf:] - qt[:, :hf])).astype(BF16)
        b11 = (g11 * jnp.where(anti, jnp.exp2(qs[hf:] - qt[:, hf:]), 0.0)).astype(BF16)
        rhs = jnp.concatenate([jnp.concatenate([f00, f01], axis=1), jnp.concatenate([zero, f11], axis=1),
                               jnp.concatenate([b00, zero], axis=1), jnp.concatenate([b10, b11], axis=1)],
                              axis=0)
        ys.append(jnp.dot(lhs, rhs, preferred_element_type=F32))
    y_t = jnp.concatenate(ys, axis=0)

    sf = sf_ref[...]
    states = jnp.concatenate([sf.astype(BF16), sball_ref[c]], axis=0)
    off = lax.dot_general(states, cm, (((1,), (1,)), ((), ())), preferred_element_type=F32)
    y_t += off[:SSD_GW] * _head_rows(jnp.exp2(q_t), 0)
    y_t += off[SSD_GW:] * _head_rows(jnp.exp2(tot_t - q_t), rk)
    wf = dt_t * jnp.exp2(tot_t - q_t)
    xw = (x_t * _head_rows(wf, 0)).astype(BF16)
    upd = jnp.dot(xw, bm, preferred_element_type=F32)
    sf_ref[...] = _head_rows(jnp.exp2(tot_t[:, :SSD_STATE]), 0) * sf + upd

    y = y_t.T + dskip_ref[...] * x_ref[...].astype(F32)
    z = z_ref[...].astype(F32)
    y = y * (z * jax.nn.sigmoid(z))
    o_ref[...] = _rmsnorm(y, nw_ref[...]).astype(BF16)


def _ssd(xbc, zx, dt_all, par, dskip, nw, layer):
    seq = xbc.shape[0]
    nc = seq // SSD_CHUNK
    t = SSD_CHUNK
    b0 = SSD_WIDTH // SSD_STATE
    c0 = b0 + SSD_GROUPS
    tri = jnp.asarray(np.tril(np.ones((t, t), np.float32)), dtype=BF16)

    def cidx(p, c):
        return p * c + (1 - p) * (nc - 1 - c)

    gps = SSD_GPS
    assert b0 % gps == 0 and c0 % gps == 0
    return pl.pallas_call(
        _ssd_body,
        grid=(SSD_GROUPS // gps, 2, nc),
        in_specs=[
            pl.BlockSpec((t, gps * SSD_GW), lambda g, p, c: (cidx(p, c), g)),
            pl.BlockSpec((t, gps * SSD_STATE), lambda g, p, c: (cidx(p, c), b0 // gps + g)),
            pl.BlockSpec((t, gps * SSD_STATE), lambda g, p, c: (p * c, c0 // gps + g)),
            pl.BlockSpec((t, gps * SSD_GW), lambda g, p, c: (p * c, g)),
            pl.BlockSpec((t, gps * LANES), lambda g, p, c: ((1 - p) * (nc - 1 - c), g)),
            pl.BlockSpec((None, gps, SUBLANES, LANES), lambda g, p, c: (layer, g, 0, 0)),
            pl.BlockSpec((None, gps, 1, SSD_GW), lambda g, p, c: (layer, g, 0, 0)),
            pl.BlockSpec((None, gps, 1, SSD_GW), lambda g, p, c: (layer, g, 0, 0)),
            pl.BlockSpec((t, t), lambda g, p, c: (0, 0)),
        ],
        out_specs=pl.BlockSpec((t, gps * SSD_GW), lambda g, p, c: (p * c, g)),
        out_shape=jax.ShapeDtypeStruct((seq, SSD_WIDTH), BF16),
        scratch_shapes=[
            pltpu.VMEM((gps, SSD_GW, SSD_STATE), F32),
            pltpu.VMEM((gps, SSD_GW, SSD_STATE), F32),
            pltpu.VMEM((gps, nc, SSD_GW, SSD_STATE), BF16),
            pltpu.VMEM((gps, nc, SSD_GW, t), F32),
            pltpu.VMEM((gps, nc, t, LANES), F32),
            pltpu.VMEM((gps, nc, 2 * SUBLANES, t), F32),
            pltpu.VMEM((gps, nc, 2 * SUBLANES, t), F32),
            pltpu.VMEM((gps, nc, 2 * SUBLANES, t), F32),
        ],
        compiler_params=_cparams("arbitrary", "arbitrary", "arbitrary"),
        name="ssd_scan",
    )(xbc, xbc, xbc, zx, dt_all, par, dskip, nw, tri)


def _fw_body(cd_ref, sd_ref, w_ref, a_ref, b_ref):
    w = w_ref[...]
    a_ref[...] = jnp.dot(cd_ref[...], w, preferred_element_type=F32, precision=HIGHEST).astype(BF16)
    b_ref[...] = jnp.dot(sd_ref[...], w, preferred_element_type=F32, precision=HIGHEST).astype(BF16)


def _fourier_weights(fourier_w, layer):
    d = FOURIER_GD
    ang = 2.0 * np.pi * np.outer(np.arange(d), np.arange(d)) / d
    cd = jnp.asarray((np.cos(ang) / np.sqrt(d)).astype(np.float32))
    sd = jnp.asarray((np.sin(ang) / np.sqrt(d)).astype(np.float32))
    return pl.pallas_call(
        _fw_body,
        grid=(FOURIER_GROUPS,),
        in_specs=[
            pl.BlockSpec((d, d), lambda g: (0, 0)),
            pl.BlockSpec((d, d), lambda g: (0, 0)),
            pl.BlockSpec((None, None, d, d), lambda g: (layer, g, 0, 0)),
        ],
        out_specs=[pl.BlockSpec((None, d, d), lambda g: (g, 0, 0))] * 2,
        out_shape=[jax.ShapeDtypeStruct((FOURIER_GROUPS, d, d), BF16)] * 2,
        compiler_params=_cparams("parallel"),
        name="fourier_weights",
    )(cd, sd, fourier_w)


DFT_NB = 4


def _dft_a_body(x_ref, f_ref, tc_ref, ts_ref, o_ref):
    n1 = DFT_N1
    y = jnp.dot(f_ref[...], x_ref[...], preferred_element_type=F32)
    reps = FOURIER_WIDTH // LANES
    for b in range(tc_ref.shape[0]):
        sl = slice(b * FOURIER_WIDTH, (b + 1) * FOURIER_WIDTH)
        yr = y[:n1, sl]
        yi = y[n1:, sl]
        tc = jnp.tile(tc_ref[b], (1, reps))
        ts = jnp.tile(ts_ref[b], (1, reps))
        o_ref[:n1, sl] = (yr * tc + yi * ts).astype(BF16)
        o_ref[n1:, sl] = (yi * tc - yr * ts).astype(BF16)


def _dft_a(u):
    seq = u.shape[0]
    n1 = DFT_N1
    n2 = seq // n1
    nb = min(DFT_NB, n2)
    ang1 = 2.0 * np.pi * np.outer(np.arange(n1), np.arange(n1)) / n1
    f1 = jnp.asarray(np.concatenate([np.cos(ang1), -np.sin(ang1)], axis=0), dtype=BF16)
    angt = 2.0 * np.pi * np.outer(np.arange(n2), np.arange(n1)) / seq
    tc = jnp.asarray(np.repeat(np.cos(angt)[:, :, None], LANES, axis=2).astype(np.float32))
    ts = jnp.asarray(np.repeat(np.sin(angt)[:, :, None], LANES, axis=2).astype(np.float32))
    x2 = u.reshape(n1, n2 * FOURIER_WIDTH)
    return pl.pallas_call(
        _dft_a_body,
        grid=(n2 // nb,),
        in_specs=[
            pl.BlockSpec((n1, nb * FOURIER_WIDTH), lambda j: (0, j)),
            pl.BlockSpec((2 * n1, n1), lambda j: (0, 0)),
            pl.BlockSpec((nb, n1, LANES), lambda j: (j, 0, 0)),
            pl.BlockSpec((nb, n1, LANES), lambda j: (j, 0, 0)),
        ],
        out_specs=pl.BlockSpec((2 * n1, nb * FOURIER_WIDTH), lambda j: (0, j)),
        out_shape=jax.ShapeDtypeStruct((2 * n1, n2 * FOURIER_WIDTH), BF16),
        compiler_params=_cparams("parallel"),
        name="dft_stage_a",
    )(x2, f1, tc, ts)


def _dft_b_body(yr_ref, yi_ref, lr_ref, li_ref, a_ref, b_ref, o_ref):
    kb, n2, width = yr_ref.shape
    rhs = jnp.concatenate([yr_ref[...].reshape(kb * n2, width), yi_ref[...].reshape(kb * n2, width)], axis=0)
    zr = jnp.dot(lr_ref[...], rhs, preferred_element_type=F32).astype(BF16)
    zi = jnp.dot(li_ref[...], rhs, preferred_element_type=F32).astype(BF16)
    outs = []
    for g in range(FOURIER_GROUPS):
        sl = slice(g * FOURIER_GD, (g + 1) * FOURIER_GD)
        outs.append(jnp.dot(zr[:, sl], a_ref[g], preferred_element_type=F32)
                    + jnp.dot(zi[:, sl], b_ref[g], preferred_element_type=F32))
    o_ref[...] = jnp.concatenate(outs, axis=1).reshape(o_ref.shape)


def _dft_b(ya, fa, fb, seq):
    n1 = DFT_N1
    n2 = seq // n1
    kb = DFT_KB
    ang2 = 2.0 * np.pi * np.outer(np.arange(n2), np.arange(n2)) / n2
    c2 = np.cos(ang2) / np.sqrt(seq)
    s2 = np.sin(ang2) / np.sqrt(seq)
    eye = np.eye(kb)
    lr = np.concatenate([np.einsum('ab,kn->kabn', eye, c2).reshape(n2 * kb, kb * n2),
                         np.einsum('ab,kn->kabn', eye, s2).reshape(n2 * kb, kb * n2)], axis=1)
    li = np.concatenate([np.einsum('ab,kn->kabn', eye, -s2).reshape(n2 * kb, kb * n2),
                         np.einsum('ab,kn->kabn', eye, c2).reshape(n2 * kb, kb * n2)], axis=1)
    y3 = ya.reshape(2 * n1, n2, FOURIER_WIDTH)
    nk = n1 // kb
    out = pl.pallas_call(
        _dft_b_body,
        grid=(nk,),
        in_specs=[
            pl.BlockSpec((kb, n2, FOURIER_WIDTH), lambda j: (j, 0, 0)),
            pl.BlockSpec((kb, n2, FOURIER_WIDTH), lambda j: (nk + j, 0, 0)),
            pl.BlockSpec((n2 * kb, 2 * kb * n2), lambda j: (0, 0)),
            pl.BlockSpec((n2 * kb, 2 * kb * n2), lambda j: (0, 0)),
            pl.BlockSpec((FOURIER_GROUPS, FOURIER_GD, FOURIER_GD), lambda j: (0, 0, 0)),
            pl.BlockSpec((FOURIER_GROUPS, FOURIER_GD, FOURIER_GD), lambda j: (0, 0, 0)),
        ],
        out_specs=pl.BlockSpec((n2, kb, FOURIER_WIDTH), lambda j: (0, j, 0)),
        out_shape=jax.ShapeDtypeStruct((n2, n1, FOURIER_WIDTH), F32),
        compiler_params=_cparams("parallel"),
        name="dft_stage_b",
    )(y3, y3, jnp.asarray(lr, dtype=BF16), jnp.asarray(li, dtype=BF16), fa, fb)
    return out.reshape(seq, FOURIER_WIDTH)


def _cast_specs(cast, n_rows, n_cols, block_index):
    src, layer = cast
    _, r, c = src.shape
    assert r % n_rows == 0 and c % n_cols == 0
    blk = (r // n_rows, c // n_cols)
    in_spec = pl.BlockSpec((None,) + blk, lambda *g: (layer,) + tuple(block_index(*g)))
    out_spec = pl.BlockSpec(blk, lambda *g: tuple(block_index(*g)))
    return in_spec, out_spec, jax.ShapeDtypeStruct((r, c), BF16)


def _outproj_body(*refs, n_lhs, cast):
    x_ref = refs[0]
    lhs = refs[1:1 + n_lhs]
    ws = refs[1 + n_lhs:1 + 2 * n_lhs]
    o_ref = refs[1 + 2 * n_lhs + int(cast)]
    acc = x_ref[...]
    for a_ref, w_ref in zip(lhs, ws):
        acc = acc + jnp.dot(a_ref[...].astype(BF16), w_ref[...], preferred_element_type=F32)
    o_ref[...] = acc
    if cast:
        refs[-1][...] = refs[1 + 2 * n_lhs][...].astype(BF16)


def _outproj(x, lhs_list, w, cast=None):
    seq = x.shape[0]
    nj, ni = D_MODEL // TN, seq // TM
    in_specs = [pl.BlockSpec((TM, TN), lambda j, i: (i, j))]
    for a in lhs_list:
        in_specs.append(pl.BlockSpec((TM, a.shape[1]), lambda j, i: (i, 0)))
    row = 0
    for a in lhs_list:
        k = a.shape[1]
        assert row % k == 0
        in_specs.append(pl.BlockSpec((k, TN), lambda j, i, rb=row // k: (rb, j)))
        row += k
    operands = [x, *lhs_list, *([w] * len(lhs_list))]
    out_specs = [pl.BlockSpec((TM, TN), lambda j, i: (i, j))]
    out_shape = [jax.ShapeDtypeStruct((seq, D_MODEL), F32)]
    if cast is not None:
        c_in, c_out, c_shape = _cast_specs(cast, ni, nj, lambda j, i: (i, j))
        in_specs.append(c_in)
        operands.append(cast[0])
        out_specs.append(c_out)
        out_shape.append(c_shape)
    outs = pl.pallas_call(
        functools.partial(_outproj_body, n_lhs=len(lhs_list), cast=cast is not None),
        grid=(nj, ni),
        in_specs=in_specs,
        out_specs=out_specs,
        out_shape=out_shape,
        compiler_params=_cparams("parallel", "parallel"),
        name="outproj",
    )(*operands)
    return outs[0], (outs[1] if cast is not None else None)


SGU_V_TILES = SGU_WIDTH // TN
SGU_TILES = 2 * SGU_V_TILES


def _sgu_body(x_ref, nw_ref, w_ref, b_ref, vnw_ref, ws_ref, bs_ref, cast_src, o_ref, cast_dst, h_ref, v_ref, ss_ref):
    j = pl.program_id(1)
    tm = x_ref.shape[0]
    cast_dst[...] = cast_src[...].astype(BF16)

    @pl.when(j == 0)
    def _():
        h_ref[...] = _rmsnorm(x_ref[...], nw_ref[...]).astype(BF16)
        ss_ref[...] = jnp.zeros_like(ss_ref)

    acc = jnp.dot(h_ref[...], w_ref[...], preferred_element_type=F32) + b_ref[...]
    act = jax.nn.gelu(acc)

    @pl.when(j < SGU_V_TILES)
    def _():
        v_ref[j] = act.astype(BF16)
        ss_ref[...] += jnp.sum(act * act, axis=-1, keepdims=True)

    @pl.when(j == SGU_V_TILES - 1)
    def _():
        rs = lax.rsqrt(ss_ref[...] * (1.0 / SGU_WIDTH) + EPS)
        per_tile = TN // SGU_GD
        for g in range(SGU_GROUPS):
            tile, off = divmod(g, per_tile)
            sl = slice(off * SGU_GD, (off + 1) * SGU_GD)
            v = v_ref[tile, :, sl].astype(F32)
            v = (v * rs * vnw_ref[:, g * SGU_GD:(g + 1) * SGU_GD]).astype(BF16)
            bias = jnp.tile(bs_ref[g], (1, SGU_GD // LANES))
            for qc in range(tm // SGU_CHUNK):
                rows = slice(qc * SGU_CHUNK, (qc + 1) * SGU_CHUNK)
                mixed = jnp.dot(ws_ref[g], v[rows], preferred_element_type=F32) + bias
                v_ref[tile, rows, sl] = mixed.astype(BF16)

    @pl.when(j >= SGU_V_TILES)
    def _():
        o_ref[...] = (act * v_ref[j - SGU_V_TILES].astype(F32)).astype(BF16)


def _sgu(x, nw, w_uv, b_uv, vnw, w_s, b_s, cast, layer, j_odd):
    seq = x.shape[0]
    tm = min(TMX, seq)
    ni = seq // tm
    wcol = lambda j: (j + SGU_V_TILES) % SGU_TILES
    c_in, c_out, c_shape = _cast_specs(cast, ni, SGU_TILES, lambda i, j: (i, j))
    return pl.pallas_call(
        _sgu_body,
        grid=(ni, SGU_TILES),
        in_specs=[
            pl.BlockSpec((tm, D_MODEL), lambda i, j: (i, 0)),
            pl.BlockSpec((None, 1, D_MODEL), lambda i, j: (layer, 0, 0)),
            pl.BlockSpec((D_MODEL, TN), lambda i, j: (0, wcol(j))),
            pl.BlockSpec((None, 1, TN), lambda i, j: (j_odd, 0, wcol(j))),
            pl.BlockSpec((None, 1, SGU_WIDTH), lambda i, j: (j_odd, 0, 0)),
            pl.BlockSpec((None, SGU_GROUPS, SGU_CHUNK, SGU_CHUNK), lambda i, j: (j_odd, 0, 0, 0)),
            pl.BlockSpec((None, SGU_GROUPS, SGU_CHUNK, LANES), lambda i, j: (j_odd, 0, 0, 0)),
            c_in,
        ],
        out_specs=[pl.BlockSpec((tm, TN), lambda i, j: (i, jnp.maximum(j - SGU_V_TILES, 0))), c_out],
        out_shape=[jax.ShapeDtypeStruct((seq, SGU_WIDTH), BF16), c_shape],
        scratch_shapes=[
            pltpu.VMEM((tm, D_MODEL), BF16),
            pltpu.VMEM((SGU_V_TILES, tm, TN), BF16),
            pltpu.VMEM((tm, 1), F32),
        ],
        compiler_params=_cparams("parallel", "arbitrary"),
        name="sgu",
    )(x, nw, w_uv, b_uv, vnw, w_s, b_s, cast[0])


def _even_in_weights(w):
    n, d = w.shape[0], w.shape[1]
    wt = jnp.swapaxes(w, 1, 2).astype(BF16)
    dt0 = ZX_WIDTH
    u0 = dt0 + 2 * SSD_HEADS
    w_dt = wt[:, dt0:u0, :].reshape(n, 2, SSD_GROUPS, SSD_RANK, d)
    w_dt = jnp.transpose(w_dt, (0, 2, 1, 3, 4)).reshape(n, SSD_GROUPS, 2 * SSD_RANK, d)
    w_dt = jnp.pad(w_dt, ((0, 0), (0, 0), (0, LANES - 2 * SSD_RANK), (0, 0))).reshape(n, SSD_GROUPS * LANES, d)
    return wt, wt[:, u0:, :], w_dt


def _group_lanes(p):
    n = p.shape[0]
    p = jnp.transpose(p.reshape(n, 2, SSD_GROUPS, SSD_RANK), (0, 2, 1, 3)).reshape(n, SSD_GROUPS, 2 * SSD_RANK)
    return jnp.pad(p, ((0, 0), (0, 0), (0, LANES - 2 * SSD_RANK)))


def kernel(x, ffn1_norm, ffn1_w_gate, ffn1_w_up, ffn1_w_down, mix_norm, ffn2_norm, ffn2_w_gate, ffn2_w_up,
           ffn2_w_down, even_w_in, ssd_conv_w, ssd_conv_b, ssd_dt_bias, ssd_a_log, ssd_d, ssd_norm, fourier_w,
           even_w_out, sgu_w_uv, sgu_b_uv, sgu_norm, sgu_w_s, sgu_b_s, odd_w_out, final_norm):
    bsz, seq, d = x.shape
    assert bsz == 1 and d == D_MODEL
    depth = ffn1_norm.shape[0]
    n_even = even_w_in.shape[0]
    xs = x.reshape(seq, d)

    row3 = lambda a: a.reshape(a.shape[0], 1, a.shape[1])
    assert depth >= 1
    ffn_f32 = {1: (ffn1_w_gate, ffn1_w_up, ffn1_w_down), 2: (ffn2_w_gate, ffn2_w_up, ffn2_w_down)}
    ffn_nw = {1: row3(ffn1_norm), 2: row3(ffn2_norm)}
    ffn_w = tuple(w[0].astype(BF16) for w in ffn_f32[1])

    def ffn(xs, which, i, ffn_w):
        last = which == 2 and i == depth - 1
        nxt = None if last else ((*ffn_f32[2], i) if which == 1 else (*ffn_f32[1], i + 1))
        return _ffn(xs, ffn_nw[which], ffn_w, i, next_w=nxt, out_norm=final_norm.reshape(1, d) if last else None)

    mixn = row3(mix_norm)
    w_in, w_in_u, w_in_dt = _even_in_weights(even_w_in)
    zeros = jnp.zeros((n_even, SSD_GROUPS, SUBLANES - 2, LANES), F32)
    par = jnp.concatenate([_group_lanes(ssd_dt_bias)[:, :, None, :], _group_lanes(ssd_a_log)[:, :, None, :], zeros],
                          axis=2)
    dskip = jnp.repeat(ssd_d, SSD_HEAD_DIM, axis=1).reshape(n_even, SSD_GROUPS, 1, SSD_GW)
    ssd_nw = ssd_norm.reshape(n_even, SSD_GROUPS, 1, SSD_GW)
    conv_b = row3(ssd_conv_b)
    b_uv = row3(sgu_b_uv)
    sgu_nw = row3(sgu_norm)
    w_s = sgu_w_s.astype(BF16)
    b_s = jnp.broadcast_to(sgu_b_s[..., None], sgu_b_s.shape + (LANES,))

    w_uv = None
    for i in range(depth):
        xs, ffn_w = ffn(xs, 1, i, ffn_w)
        j = i // 2
        if i % 2 == 0:
            zx, u, dt_all, w_out = _inproj(xs, mixn, w_in, w_in_u, w_in_dt, (even_w_out, j), i, j)
            xbc = _conv(zx, ssd_conv_w, conv_b, j)
            y_ssd = _ssd(xbc, zx, dt_all, par, dskip, ssd_nw, j)
            fa, fb = _fourier_weights(fourier_w, j)
            y_fft = _dft_b(_dft_a(u), fa, fb, seq)
            nxt = (sgu_w_uv, j) if i + 1 < depth else None
            xs, w_uv = _outproj(xs, [y_ssd, y_fft], w_out, cast=nxt)
        else:
            gated, w_out = _sgu(xs, mixn, w_uv, b_uv, sgu_nw, w_s, b_s, (odd_w_out, j), i, j)
            xs, _ = _outproj(xs, [gated], w_out)
        xs, ffn_w = ffn(xs, 2, i, ffn_w)
    return xs.reshape(bsz, seq, d)
```

```python
import functools

import numpy as np
import jax
import jax.numpy as jnp
from jax import lax
from jax.experimental import pallas as pl
from jax.experimental.pallas import tpu as pltpu

F32 = jnp.float32
BF16 = jnp.bfloat16
HIGHEST = lax.Precision.HIGHEST
LOG2E = 1.4426950408889634

D_MODEL = 2048
D_FF = 5632
EPS = 1e-6
SSD_HEAD_DIM = 64
SSD_HEADS = 48
SSD_GROUPS = 8
SSD_RANK = SSD_HEADS // SSD_GROUPS
SSD_GW = SSD_RANK * SSD_HEAD_DIM
SSD_STATE = 128
SSD_CHUNK = 256
SSD_WIDTH = SSD_HEADS * SSD_HEAD_DIM
SSD_CONV = 5
SSD_CONV_CH = SSD_WIDTH + 2 * SSD_GROUPS * SSD_STATE
FOURIER_WIDTH = 1024
FOURIER_GROUPS = 4
FOURIER_GD = FOURIER_WIDTH // FOURIER_GROUPS
SGU_WIDTH = 4096
SGU_GROUPS = 8
SGU_GD = SGU_WIDTH // SGU_GROUPS
SGU_CHUNK = 128
ZX_WIDTH = SSD_WIDTH + SSD_CONV_CH

LANES = 128
SUBLANES = 8
VMEM_LIMIT = 56 * 1024 * 1024
VMEM_LIMIT_BIG = 60 * 1024 * 1024

TM = 512
TMX = 1024
TF = 512
TN = 1024
SSD_GPS = 2
DFT_N1 = 128
DFT_KB = 8


def _cparams(*sem, vmem_limit=VMEM_LIMIT):
    return pltpu.CompilerParams(dimension_semantics=sem, vmem_limit_bytes=vmem_limit)


def _rmsnorm(x, w):
    ms = jnp.mean(x * x, axis=-1, keepdims=True)
    return x * lax.rsqrt(ms + EPS) * w


def _ffn_body(*refs, cast_next, out_norm):
    x_ref, nw_ref, wg_ref, wu_ref, wd_ref = refs[:5]
    pos = 5
    nxt_in = refs[pos:pos + 3] if cast_next else ()
    pos += len(nxt_in)
    out_nw_ref = refs[pos] if out_norm else None
    pos += int(out_norm)
    o_ref = refs[pos]
    nxt_out = refs[pos + 1:pos + 1 + len(nxt_in)]
    xn_ref = refs[-1]
    j = pl.program_id(1)

    def half_swiglu():
        xn = xn_ref[...]
        g = jnp.dot(xn, wg_ref[...], preferred_element_type=F32)
        u = jnp.dot(xn, wu_ref[...], preferred_element_type=F32)
        h = ((0.5 * g) * jax.nn.sigmoid(g) * u).astype(BF16)
        return jnp.dot(h, wd_ref[...], preferred_element_type=F32)

    def cast_next():
        for src, dst in zip(nxt_in, nxt_out):
            dst[...] = src[...].astype(BF16)

    @pl.when(j == 0)
    def _():
        xn_ref[...] = _rmsnorm(x_ref[...], nw_ref[...]).astype(BF16)
        o_ref[...] = x_ref[...] + half_swiglu()
        cast_next()

    @pl.when(j > 0)
    def _():
        o_ref[...] += half_swiglu()
        cast_next()

    if out_norm:
        @pl.when(j == pl.num_programs(1) - 1)
        def _():
            o_ref[...] = _rmsnorm(o_ref[...], out_nw_ref[...])


def _ffn(x, nw, w, layer, next_w=None, out_norm=None):
    seq = x.shape[0]
    tm = min(TMX, seq)
    ni, nj = seq // tm, D_FF // TF
    in_specs = [
        pl.BlockSpec((tm, D_MODEL), lambda i, j: (i, 0)),
        pl.BlockSpec((None, 1, D_MODEL), lambda i, j: (layer, 0, 0)),
        pl.BlockSpec((D_MODEL, TF), lambda i, j: (0, j)),
        pl.BlockSpec((D_MODEL, TF), lambda i, j: (0, j)),
        pl.BlockSpec((TF, D_MODEL), lambda i, j: (j, 0)),
    ]
    operands = [x, nw, *w]
    out_specs = [pl.BlockSpec((tm, D_MODEL), lambda i, j: (i, 0))]
    out_shape = [jax.ShapeDtypeStruct((seq, D_MODEL), F32)]
    if next_w is not None:
        g32, u32, d32, nl = next_w
        assert D_MODEL % ni == 0
        rb = D_MODEL // ni
        in_specs += [pl.BlockSpec((None, rb, TF), lambda i, j: (nl, i, j)),
                     pl.BlockSpec((None, rb, TF), lambda i, j: (nl, i, j)),
                     pl.BlockSpec((None, TF, rb), lambda i, j: (nl, j, i))]
        operands += [g32, u32, d32]
        out_specs += [pl.BlockSpec((rb, TF), lambda i, j: (i, j)),
                      pl.BlockSpec((rb, TF), lambda i, j: (i, j)),
                      pl.BlockSpec((TF, rb), lambda i, j: (j, i))]
        out_shape += [jax.ShapeDtypeStruct((D_MODEL, D_FF), BF16),
                      jax.ShapeDtypeStruct((D_MODEL, D_FF), BF16),
                      jax.ShapeDtypeStruct((D_FF, D_MODEL), BF16)]
    if out_norm is not None:
        in_specs.append(pl.BlockSpec((1, D_MODEL), lambda i, j: (0, 0)))
        operands.append(out_norm)
    outs = pl.pallas_call(
        functools.partial(_ffn_body, cast_next=next_w is not None, out_norm=out_norm is not None),
        grid=(ni, nj),
        in_specs=in_specs,
        out_specs=out_specs,
        out_shape=out_shape,
        scratch_shapes=[pltpu.VMEM((tm, D_MODEL), BF16)],
        compiler_params=_cparams("parallel", "arbitrary", vmem_limit=VMEM_LIMIT_BIG),
        name="ffn",
    )(*operands)
    return outs[0], tuple(outs[1:])


N_ZX_TILES = ZX_WIDTH // TN
IN_TILES = N_ZX_TILES + 2


def _inproj_body(x_ref, nw_ref, w_ref, wu_ref, wdt_ref, cast_src, zx_ref, u_ref, dt_ref, cast_dst, h_ref):
    j = pl.program_id(1)
    cast_dst[...] = cast_src[...].astype(BF16)

    @pl.when(j == 0)
    def _():
        h_ref[...] = _rmsnorm(x_ref[...], nw_ref[...]).astype(BF16)

    @pl.when(j < N_ZX_TILES)
    def _():
        zx_ref[...] = lax.dot_general(h_ref[...], w_ref[...], (((1,), (1,)), ((), ())),
                                      preferred_element_type=F32).astype(BF16)

    @pl.when(j == N_ZX_TILES)
    def _():
        u_ref[...] = lax.dot_general(h_ref[...], wu_ref[...], (((1,), (1,)), ((), ())),
                                     preferred_element_type=F32).astype(BF16)

    @pl.when(j == N_ZX_TILES + 1)
    def _():
        dt_ref[...] = lax.dot_general(h_ref[...], wdt_ref[...], (((1,), (1,)), ((), ())),
                                      preferred_element_type=F32)


def _inproj(x, nw, w, w_u, w_dt, cast, layer, j_even):
    seq = x.shape[0]
    tm = min(TMX, seq)
    ni = seq // tm
    once = pl.Buffered(1)
    c_in, c_out, c_shape = _cast_specs(cast, ni, N_ZX_TILES, lambda i, j: (i, jnp.minimum(j, N_ZX_TILES - 1)))
    return pl.pallas_call(
        _inproj_body,
        grid=(ni, IN_TILES),
        in_specs=[
            pl.BlockSpec((tm, D_MODEL), lambda i, j: (i, 0)),
            pl.BlockSpec((None, 1, D_MODEL), lambda i, j: (layer, 0, 0)),
            pl.BlockSpec((None, TN, D_MODEL), lambda i, j: (j_even, jnp.minimum(j, N_ZX_TILES - 1), 0)),
            pl.BlockSpec((None, FOURIER_WIDTH, D_MODEL), lambda i, j: (j_even, 0, 0), pipeline_mode=once),
            pl.BlockSpec((None, SSD_GROUPS * LANES, D_MODEL), lambda i, j: (j_even, 0, 0), pipeline_mode=once),
            c_in,
        ],
        out_specs=[
            pl.BlockSpec((tm, TN), lambda i, j: (i, jnp.minimum(j, N_ZX_TILES - 1))),
            pl.BlockSpec((tm, FOURIER_WIDTH), lambda i, j: (i, 0)),
            pl.BlockSpec((tm, SSD_GROUPS * LANES), lambda i, j: (i, 0)),
            c_out,
        ],
        out_shape=[
            jax.ShapeDtypeStruct((seq, ZX_WIDTH), BF16),
            jax.ShapeDtypeStruct((seq, FOURIER_WIDTH), BF16),
            jax.ShapeDtypeStruct((seq, SSD_GROUPS * LANES), F32),
            c_shape,
        ],
        scratch_shapes=[pltpu.VMEM((tm, D_MODEL), BF16)],
        compiler_params=_cparams("parallel", "arbitrary", vmem_limit=VMEM_LIMIT_BIG),
        name="even_inproj",
    )(x, nw, w, w_u, w_dt, cast[0])


CONV_TR = 2048
CONV_TC = 512
CONV_HALO = 16


def _conv_body(xm_ref, xp_ref, xn_ref, w_ref, b_ref, o_ref):
    i = pl.program_id(0)
    last = pl.num_programs(0) - 1
    tr = xm_ref.shape[0]
    prev = xp_ref[...].astype(F32)[CONV_HALO - SUBLANES:]
    nxt = xn_ref[...].astype(F32)[:SUBLANES]
    ext = jnp.concatenate([jnp.where(i == 0, 0.0, prev), xm_ref[...].astype(F32), jnp.where(i == last, 0.0, nxt)],
                          axis=0)
    n = tr + 2 * SUBLANES
    w = w_ref[...]
    acc = jnp.broadcast_to(b_ref[...], o_ref.shape)
    half = SSD_CONV // 2
    for k in range(SSD_CONV):
        shifted = ext if k == half else pltpu.roll(ext, (half - k) % n, axis=0)
        acc = acc + shifted[SUBLANES:SUBLANES + tr, :] * w[k:k + 1, :]
    o_ref[...] = (acc * jax.nn.sigmoid(acc)).astype(BF16)


def _conv(zx, conv_w, conv_b, layer):
    seq = zx.shape[0]
    tr = min(CONV_TR, seq)
    col0 = SSD_WIDTH // CONV_TC
    hb = tr // CONV_HALO
    nhb = seq // CONV_HALO
    return pl.pallas_call(
        _conv_body,
        grid=(seq // tr, SSD_CONV_CH // CONV_TC),
        in_specs=[
            pl.BlockSpec((tr, CONV_TC), lambda i, j: (i, col0 + j)),
            pl.BlockSpec((CONV_HALO, CONV_TC), lambda i, j: (jnp.maximum(i * hb - 1, 0), col0 + j)),
            pl.BlockSpec((CONV_HALO, CONV_TC), lambda i, j: (jnp.minimum((i + 1) * hb, nhb - 1), col0 + j)),
            pl.BlockSpec((None, SSD_CONV, CONV_TC), lambda i, j: (layer, 0, j)),
            pl.BlockSpec((None, 1, CONV_TC), lambda i, j: (layer, 0, j)),
        ],
        out_specs=pl.BlockSpec((tr, CONV_TC), lambda i, j: (i, j)),
        out_shape=jax.ShapeDtypeStruct((seq, SSD_CONV_CH), BF16),
        compiler_params=_cparams("parallel", "parallel"),
        name="ssd_conv",
    )(zx, zx, zx, conv_w, conv_b)


def _softplus(v):
    return jnp.maximum(v, 0.0) + jnp.log1p(jnp.exp(-jnp.abs(v)))


def _head_rows(rows, first):
    n = rows.shape[1]
    return jnp.concatenate([jnp.broadcast_to(rows[first + h:first + h + 1, :], (SSD_HEAD_DIM, n))
                            for h in range(SSD_RANK)], axis=0)


def _ssd_body(x_ref, b_ref, c_ref, z_ref, dt_ref, par_ref, dskip_ref, nw_ref, tri_ref, o_ref, *scratch):
    phase = pl.program_id(1)
    c = pl.program_id(2)
    nc = pl.num_programs(2)
    sf_ref, sb_ref = scratch[:2]

    def group(gi):
        cols = lambda w: slice(gi * w, (gi + 1) * w)
        return dict(x_ref=x_ref.at[:, cols(SSD_GW)], b_ref=b_ref.at[:, cols(SSD_STATE)],
                    c_ref=c_ref.at[:, cols(SSD_STATE)], z_ref=z_ref.at[:, cols(SSD_GW)],
                    dt_ref=dt_ref.at[:, cols(LANES)], par_ref=par_ref.at[gi], dskip_ref=dskip_ref.at[gi],
                    nw_ref=nw_ref.at[gi], tri_ref=tri_ref, o_ref=o_ref.at[:, cols(SSD_GW)],
                    scratch=[s.at[gi] for s in scratch])

    @pl.when(phase == 0)
    def _():
        @pl.when(c == 0)
        def _():
            sb_ref[...] = jnp.zeros_like(sb_ref)

        for gi in range(SSD_GPS):
            _ssd_prepare(nc - 1 - c, **group(gi))

    @pl.when(phase == 1)
    def _():
        @pl.when(c == 0)
        def _():
            sf_ref[...] = jnp.zeros_like(sf_ref)

        for gi in range(SSD_GPS):
            _ssd_emit(c, **group(gi))


def _ssd_prepare(cc, x_ref, b_ref, dt_ref, par_ref, tri_ref, scratch, **unused):
    _, sb_ref, sball_ref, xt_ref, qn_ref, dtt_ref, qt_ref, tott_ref = scratch
    t = SSD_CHUNK
    rk = SSD_RANK
    prm = dtt_ref.shape[1]
    bm = b_ref[...]
    par = par_ref[...]
    lane = lax.broadcasted_iota(jnp.int32, (1, LANES), 1)
    dt = _softplus(dt_ref[...] + par[0:1, :])
    da = dt * (-jnp.exp(par[1:2, :]))
    d1 = da.astype(BF16)
    r1 = da - d1.astype(F32)
    d2 = r1.astype(BF16)
    d3 = (r1 - d2.astype(F32)).astype(BF16)
    cs3 = jnp.dot(tri_ref[...], jnp.concatenate([d1, d2, d3], axis=1), preferred_element_type=F32)
    cs = cs3[:, :LANES] + cs3[:, LANES:2 * LANES] + cs3[:, 2 * LANES:]
    q = (cs - jnp.where(lane >= rk, da, 0.0)) * LOG2E
    dt_t = dt.T[:prm]
    q_t = q.T[:prm]
    tot_t = jnp.broadcast_to((cs * LOG2E).T[:prm, t - 1:t], (prm, t))
    x_t = x_ref[...].astype(F32).T
    xt_ref[cc] = x_t
    qn_ref[cc] = q
    dtt_ref[cc] = dt_t
    qt_ref[cc] = q_t
    tott_ref[cc] = tot_t

    sb = sb_ref[...]
    sball_ref[cc] = sb.astype(BF16)
    wb = dt_t * jnp.exp2(q_t)
    xw = (x_t * _head_rows(wb, rk)).astype(BF16)
    upd = jnp.dot(xw, bm, preferred_element_type=F32)
    sb_ref[...] = _head_rows(jnp.exp2(tot_t[:, :SSD_STATE]), rk) * sb + upd


def _ssd_emit(c, x_ref, b_ref, c_ref, z_ref, dskip_ref, nw_ref, o_ref, scratch, **unused):
    sf_ref, _, sball_ref, xt_ref, qn_ref, dtt_ref, qt_ref, tott_ref = scratch
    t = SSD_CHUNK
    rk = SSD_RANK
    hd = SSD_HEAD_DIM
    bm = b_ref[...]
    x_t = xt_ref[c]
    q = qn_ref[c]
    dt_t = dtt_ref[c]
    q_t = qt_ref[c]
    tot_t = tott_ref[c]
    cm = c_ref[...]
    gt = lax.dot_general(bm, cm, (((1,), (1,)), ((), ())), preferred_element_type=F32)
    hf = t // 2
    srow = lax.broadcasted_iota(jnp.int32, (hf, hf), 0)
    tcol = lax.broadcasted_iota(jnp.int32, (hf, hf), 1)
    causal = srow <= tcol
    anti = srow >= tcol
    zero = jnp.zeros((hf, hf), BF16)
    g00, g01, g10, g11 = gt[:hf, :hf], gt[:hf, hf:], gt[hf:, :hf], gt[hf:, hf:]
    ys = []
    for h in range(rk):
        hb = rk + h
        xh = x_t[h * hd:(h + 1) * hd, :]
        lhs = jnp.concatenate([(xh * dt_t[h:h + 1, :]).astype(BF16), (xh * dt_t[hb:hb + 1, :]).astype(BF16)],
                              axis=1)
        qs, qt = q[:, h:h + 1], q_t[h:h + 1, :]
        f00 = (g00 * jnp.where(causal, jnp.exp2(qt[:, :hf] - qs[:hf]), 0.0)).astype(BF16)
        f01 = (g01 * jnp.exp2(qt[:, hf:] - qs[:hf])).astype(BF16)
        f11 = (g11 * jnp.where(causal, jnp.exp2(qt[:, hf:] - qs[hf:]), 0.0)).astype(BF16)
        qs, qt = q[:, hb:hb + 1], q_t[hb:hb + 1, :]
        b00 = (g00 * jnp.where(anti, jnp.exp2(qs[:hf] - qt[:, :hf]), 0.0)).astype(BF16)
        b10 = (g10 * jnp.exp2(qs[hf:] - qt[:, :hf])).astype(BF16)
        b11 = (g11 * jnp.where(anti, jnp.exp2(qs[hf:] - qt[:, hf:]), 0.0)).astype(BF16)
        rhs = jnp.concatenate([jnp.concatenate([f00, f01], axis=1), jnp.concatenate([zero, f11], axis=1),
                               jnp.concatenate([b00, zero], axis=1), jnp.concatenate([b10, b11], axis=1)],
                              axis=0)
        ys.append(jnp.dot(lhs, rhs, preferred_element_type=F32))
    y_t = jnp.concatenate(ys, axis=0)

    sf = sf_ref[...]
    states = jnp.concatenate([sf.astype(BF16), sball_ref[c]], axis=0)
    off = lax.dot_general(states, cm, (((1,), (1,)), ((), ())), preferred_element_type=F32)
    y_t += off[:SSD_GW] * _head_rows(jnp.exp2(q_t), 0)
    y_t += off[SSD_GW:] * _head_rows(jnp.exp2(tot_t - q_t), rk)
    wf = dt_t * jnp.exp2(tot_t - q_t)
    xw = (x_t * _head_rows(wf, 0)).astype(BF16)
    upd = jnp.dot(xw, bm, preferred_element_type=F32)
    sf_ref[...] = _head_rows(jnp.exp2(tot_t[:, :SSD_STATE]), 0) * sf + upd

    y = y_t.T + dskip_ref[...] * x_ref[...].astype(F32)
    z = z_ref[...].astype(F32)
    y = y * (z * jax.nn.sigmoid(z))
    o_ref[...] = _rmsnorm(y, nw_ref[...]).astype(BF16)


def _ssd(xbc, zx, dt_all, par, dskip, nw, layer):
    seq = xbc.shape[0]
    nc = seq // SSD_CHUNK
    t = SSD_CHUNK
    b0 = SSD_WIDTH // SSD_STATE
    c0 = b0 + SSD_GROUPS
    tri = jnp.asarray(np.tril(np.ones((t, t), np.float32)), dtype=BF16)

    def cidx(p, c):
        return p * c + (1 - p) * (nc - 1 - c)

    gps = SSD_GPS
    assert b0 % gps == 0 and c0 % gps == 0
    return pl.pallas_call(
        _ssd_body,
        grid=(SSD_GROUPS // gps, 2, nc),
        in_specs=[
            pl.BlockSpec((t, gps * SSD_GW), lambda g, p, c: (cidx(p, c), g)),
            pl.BlockSpec((t, gps * SSD_STATE), lambda g, p, c: (cidx(p, c), b0 // gps + g)),
            pl.BlockSpec((t, gps * SSD_STATE), lambda g, p, c: (p * c, c0 // gps + g)),
            pl.BlockSpec((t, gps * SSD_GW), lambda g, p, c: (p * c, g)),
            pl.BlockSpec((t, gps * LANES), lambda g, p, c: ((1 - p) * (nc - 1 - c), g)),
            pl.BlockSpec((None, gps, SUBLANES, LANES), lambda g, p, c: (layer, g, 0, 0)),
            pl.BlockSpec((None, gps, 1, SSD_GW), lambda g, p, c: (layer, g, 0, 0)),
            pl.BlockSpec((None, gps, 1, SSD_GW), lambda g, p, c: (layer, g, 0, 0)),
            pl.BlockSpec((t, t), lambda g, p, c: (0, 0)),
        ],
        out_specs=pl.BlockSpec((t, gps * SSD_GW), lambda g, p, c: (p * c, g)),
        out_shape=jax.ShapeDtypeStruct((seq, SSD_WIDTH), BF16),
        scratch_shapes=[
            pltpu.VMEM((gps, SSD_GW, SSD_STATE), F32),
            pltpu.VMEM((gps, SSD_GW, SSD_STATE), F32),
            pltpu.VMEM((gps, nc, SSD_GW, SSD_STATE), BF16),
            pltpu.VMEM((gps, nc, SSD_GW, t), F32),
            pltpu.VMEM((gps, nc, t, LANES), F32),
            pltpu.VMEM((gps, nc, 2 * SUBLANES, t), F32),
            pltpu.VMEM((gps, nc, 2 * SUBLANES, t), F32),
            pltpu.VMEM((gps, nc, 2 * SUBLANES, t), F32),
        ],
        compiler_params=_cparams("arbitrary", "arbitrary", "arbitrary"),
        name="ssd_scan",
    )(xbc, xbc, xbc, zx, dt_all, par, dskip, nw, tri)


def _fw_body(cd_ref, sd_ref, w_ref, a_ref, b_ref):
    w = w_ref[...]
    a_ref[...] = jnp.dot(cd_ref[...], w, preferred_element_type=F32, precision=HIGHEST).astype(BF16)
    b_ref[...] = jnp.dot(sd_ref[...], w, preferred_element_type=F32, precision=HIGHEST).astype(BF16)


def _fourier_weights(fourier_w, layer):
    d = FOURIER_GD
    ang = 2.0 * np.pi * np.outer(np.arange(d), np.arange(d)) / d
    cd = jnp.asarray((np.cos(ang) / np.sqrt(d)).astype(np.float32))
    sd = jnp.asarray((np.sin(ang) / np.sqrt(d)).astype(np.float32))
    return pl.pallas_call(
        _fw_body,
        grid=(FOURIER_GROUPS,),
        in_specs=[
            pl.BlockSpec((d, d), lambda g: (0, 0)),
            pl.BlockSpec((d, d), lambda g: (0, 0)),
            pl.BlockSpec((None, None, d, d), lambda g: (layer, g, 0, 0)),
        ],
        out_specs=[pl.BlockSpec((None, d, d), lambda g: (g, 0, 0))] * 2,
        out_shape=[jax.ShapeDtypeStruct((FOURIER_GROUPS, d, d), BF16)] * 2,
        compiler_params=_cparams("parallel"),
        name="fourier_weights",
    )(cd, sd, fourier_w)


DFT_NB = 4


def _dft_a_body(x_ref, f_ref, tc_ref, ts_ref, o_ref):
    n1 = DFT_N1
    y = jnp.dot(f_ref[...], x_ref[...], preferred_element_type=F32)
    reps = FOURIER_WIDTH // LANES
    for b in range(tc_ref.shape[0]):
        sl = slice(b * FOURIER_WIDTH, (b + 1) * FOURIER_WIDTH)
        yr = y[:n1, sl]
        yi = y[n1:, sl]
        tc = jnp.tile(tc_ref[b], (1, reps))
        ts = jnp.tile(ts_ref[b], (1, reps))
        o_ref[:n1, sl] = (yr * tc + yi * ts).astype(BF16)
        o_ref[n1:, sl] = (yi * tc - yr * ts).astype(BF16)


def _dft_a(u):
    seq = u.shape[0]
    n1 = DFT_N1
    n2 = seq // n1
    nb = min(DFT_NB, n2)
    ang1 = 2.0 * np.pi * np.outer(np.arange(n1), np.arange(n1)) / n1
    f1 = jnp.asarray(np.concatenate([np.cos(ang1), -np.sin(ang1)], axis=0), dtype=BF16)
    angt = 2.0 * np.pi * np.outer(np.arange(n2), np.arange(n1)) / seq
    tc = jnp.asarray(np.repeat(np.cos(angt)[:, :, None], LANES, axis=2).astype(np.float32))
    ts = jnp.asarray(np.repeat(np.sin(angt)[:, :, None], LANES, axis=2).astype(np.float32))
    x2 = u.reshape(n1, n2 * FOURIER_WIDTH)
    return pl.pallas_call(
        _dft_a_body,
        grid=(n2 // nb,),
        in_specs=[
            pl.BlockSpec((n1, nb * FOURIER_WIDTH), lambda j: (0, j)),
            pl.BlockSpec((2 * n1, n1), lambda j: (0, 0)),
            pl.BlockSpec((nb, n1, LANES), lambda j: (j, 0, 0)),
            pl.BlockSpec((nb, n1, LANES), lambda j: (j, 0, 0)),
        ],
        out_specs=pl.BlockSpec((2 * n1, nb * FOURIER_WIDTH), lambda j: (0, j)),
        out_shape=jax.ShapeDtypeStruct((2 * n1, n2 * FOURIER_WIDTH), BF16),
        compiler_params=_cparams("parallel"),
        name="dft_stage_a",
    )(x2, f1, tc, ts)


def _dft_b_body(yr_ref, yi_ref, lr_ref, li_ref, a_ref, b_ref, o_ref):
    kb, n2, width = yr_ref.shape
    rhs = jnp.concatenate([yr_ref[...].reshape(kb * n2, width), yi_ref[...].reshape(kb * n2, width)], axis=0)
    zr = jnp.dot(lr_ref[...], rhs, preferred_element_type=F32).astype(BF16)
    zi = jnp.dot(li_ref[...], rhs, preferred_element_type=F32).astype(BF16)
    outs = []
    for g in range(FOURIER_GROUPS):
        sl = slice(g * FOURIER_GD, (g + 1) * FOURIER_GD)
        outs.append(jnp.dot(zr[:, sl], a_ref[g], preferred_element_type=F32)
                    + jnp.dot(zi[:, sl], b_ref[g], preferred_element_type=F32))
    o_ref[...] = jnp.concatenate(outs, axis=1).reshape(o_ref.shape)


def _dft_b(ya, fa, fb, seq):
    n1 = DFT_N1
    n2 = seq // n1
    kb = DFT_KB
    ang2 = 2.0 * np.pi * np.outer(np.arange(n2), np.arange(n2)) / n2
    c2 = np.cos(ang2) / np.sqrt(seq)
    s2 = np.sin(ang2) / np.sqrt(seq)
    eye = np.eye(kb)
    lr = np.concatenate([np.einsum('ab,kn->kabn', eye, c2).reshape(n2 * kb, kb * n2),
                         np.einsum('ab,kn->kabn', eye, s2).reshape(n2 * kb, kb * n2)], axis=1)
    li = np.concatenate([np.einsum('ab,kn->kabn', eye, -s2).reshape(n2 * kb, kb * n2),
                         np.einsum('ab,kn->kabn', eye, c2).reshape(n2 * kb, kb * n2)], axis=1)
    y3 = ya.reshape(2 * n1, n2, FOURIER_WIDTH)
    nk = n1 // kb
    out = pl.pallas_call(
        _dft_b_body,
        grid=(nk,),
        in_specs=[
            pl.BlockSpec((kb, n2, FOURIER_WIDTH), lambda j: (j, 0, 0)),
            pl.BlockSpec((kb, n2, FOURIER_WIDTH), lambda j: (nk + j, 0, 0)),
            pl.BlockSpec((n2 * kb, 2 * kb * n2), lambda j: (0, 0)),
            pl.BlockSpec((n2 * kb, 2 * kb * n2), lambda j: (0, 0)),
            pl.BlockSpec((FOURIER_GROUPS, FOURIER_GD, FOURIER_GD), lambda j: (0, 0, 0)),
            pl.BlockSpec((FOURIER_GROUPS, FOURIER_GD, FOURIER_GD), lambda j: (0, 0, 0)),
        ],
        out_specs=pl.BlockSpec((n2, kb, FOURIER_WIDTH), lambda j: (0, j, 0)),
        out_shape=jax.ShapeDtypeStruct((n2, n1, FOURIER_WIDTH), F32),
        compiler_params=_cparams("parallel"),
        name="dft_stage_b",
    )(y3, y3, jnp.asarray(lr, dtype=BF16), jnp.asarray(li, dtype=BF16), fa, fb)
    return out.reshape(seq, FOURIER_WIDTH)


def _cast_specs(cast, n_rows, n_cols, block_index):
    src, layer = cast
    _, r, c = src.shape
    assert r % n_rows == 0 and c % n_cols == 0
    blk = (r // n_rows, c // n_cols)
    in_spec = pl.BlockSpec((None,) + blk, lambda *g: (layer,) + tuple(block_index(*g)))
    out_spec = pl.BlockSpec(blk, lambda *g: tuple(block_index(*g)))
    return in_spec, out_spec, jax.ShapeDtypeStruct((r, c), BF16)


def _outproj_body(*refs, n_lhs, cast):
    x_ref = refs[0]
    lhs = refs[1:1 + n_lhs]
    ws = refs[1 + n_lhs:1 + 2 * n_lhs]
    o_ref = refs[1 + 2 * n_lhs + int(cast)]
    acc = x_ref[...]
    for a_ref, w_ref in zip(lhs, ws):
        acc = acc + jnp.dot(a_ref[...].astype(BF16), w_ref[...], preferred_element_type=F32)
    o_ref[...] = acc
    if cast:
        refs[-1][...] = refs[1 + 2 * n_lhs][...].astype(BF16)


def _outproj(x, lhs_list, w, cast=None):
    seq = x.shape[0]
    nj, ni = D_MODEL // TN, seq // TM
    in_specs = [pl.BlockSpec((TM, TN), lambda j, i: (i, j))]
    for a in lhs_list:
        in_specs.append(pl.BlockSpec((TM, a.shape[1]), lambda j, i: (i, 0)))
    row = 0
    for a in lhs_list:
        k = a.shape[1]
        assert row % k == 0
        in_specs.append(pl.BlockSpec((k, TN), lambda j, i, rb=row // k: (rb, j)))
        row += k
    operands = [x, *lhs_list, *([w] * len(lhs_list))]
    out_specs = [pl.BlockSpec((TM, TN), lambda j, i: (i, j))]
    out_shape = [jax.ShapeDtypeStruct((seq, D_MODEL), F32)]
    if cast is not None:
        c_in, c_out, c_shape = _cast_specs(cast, ni, nj, lambda j, i: (i, j))
        in_specs.append(c_in)
        operands.append(cast[0])
        out_specs.append(c_out)
        out_shape.append(c_shape)
    outs = pl.pallas_call(
        functools.partial(_outproj_body, n_lhs=len(lhs_list), cast=cast is not None),
        grid=(nj, ni),
        in_specs=in_specs,
        out_specs=out_specs,
        out_shape=out_shape,
        compiler_params=_cparams("parallel", "parallel"),
        name="outproj",
    )(*operands)
    return outs[0], (outs[1] if cast is not None else None)


SGU_V_TILES = SGU_WIDTH // TN
SGU_TILES = 2 * SGU_V_TILES


def _sgu_body(x_ref, nw_ref, w_ref, b_ref, vnw_ref, ws_ref, bs_ref, cast_src, o_ref, cast_dst, h_ref, v_ref, ss_ref):
    j = pl.program_id(1)
    tm = x_ref.shape[0]
    cast_dst[...] = cast_src[...].astype(BF16)

    @pl.when(j == 0)
    def _():
        h_ref[...] = _rmsnorm(x_ref[...], nw_ref[...]).astype(BF16)
        ss_ref[...] = jnp.zeros_like(ss_ref)

    acc = jnp.dot(h_ref[...], w_ref[...], preferred_element_type=F32) + b_ref[...]
    act = jax.nn.gelu(acc)

    @pl.when(j < SGU_V_TILES)
    def _():
        v_ref[j] = act.astype(BF16)
        ss_ref[...] += jnp.sum(act * act, axis=-1, keepdims=True)

    @pl.when(j == SGU_V_TILES - 1)
    def _():
        rs = lax.rsqrt(ss_ref[...] * (1.0 / SGU_WIDTH) + EPS)
        per_tile = TN // SGU_GD
        for g in range(SGU_GROUPS):
            tile, off = divmod(g, per_tile)
            sl = slice(off * SGU_GD, (off + 1) * SGU_GD)
            v = v_ref[tile, :, sl].astype(F32)
            v = (v * rs * vnw_ref[:, g * SGU_GD:(g + 1) * SGU_GD]).astype(BF16)
            bias = jnp.tile(bs_ref[g], (1, SGU_GD // LANES))
            for qc in range(tm // SGU_CHUNK):
                rows = slice(qc * SGU_CHUNK, (qc + 1) * SGU_CHUNK)
                mixed = jnp.dot(ws_ref[g], v[rows], preferred_element_type=F32) + bias
                v_ref[tile, rows, sl] = mixed.astype(BF16)

    @pl.when(j >= SGU_V_TILES)
    def _():
        o_ref[...] = (act * v_ref[j - SGU_V_TILES].astype(F32)).astype(BF16)


def _sgu(x, nw, w_uv, b_uv, vnw, w_s, b_s, cast, layer, j_odd):
    seq = x.shape[0]
    tm = min(TMX, seq)
    ni = seq // tm
    wcol = lambda j: (j + SGU_V_TILES) % SGU_TILES
    c_in, c_out, c_shape = _cast_specs(cast, ni, SGU_TILES, lambda i, j: (i, j))
    return pl.pallas_call(
        _sgu_body,
        grid=(ni, SGU_TILES),
        in_specs=[
            pl.BlockSpec((tm, D_MODEL), lambda i, j: (i, 0)),
            pl.BlockSpec((None, 1, D_MODEL), lambda i, j: (layer, 0, 0)),
            pl.BlockSpec((D_MODEL, TN), lambda i, j: (0, wcol(j))),
            pl.BlockSpec((None, 1, TN), lambda i, j: (j_odd, 0, wcol(j))),
            pl.BlockSpec((None, 1, SGU_WIDTH), lambda i, j: (j_odd, 0, 0)),
            pl.BlockSpec((None, SGU_GROUPS, SGU_CHUNK, SGU_CHUNK), lambda i, j: (j_odd, 0, 0, 0)),
            pl.BlockSpec((None, SGU_GROUPS, SGU_CHUNK, LANES), lambda i, j: (j_odd, 0, 0, 0)),
            c_in,
        ],
        out_specs=[pl.BlockSpec((tm, TN), lambda i, j: (i, jnp.maximum(j - SGU_V_TILES, 0))), c_out],
        out_shape=[jax.ShapeDtypeStruct((seq, SGU_WIDTH), BF16), c_shape],
        scratch_shapes=[
            pltpu.VMEM((tm, D_MODEL), BF16),
            pltpu.VMEM((SGU_V_TILES, tm, TN), BF16),
            pltpu.VMEM((tm, 1), F32),
        ],
        compiler_params=_cparams("parallel", "arbitrary"),
        name="sgu",
    )(x, nw, w_uv, b_uv, vnw, w_s, b_s, cast[0])


def _even_in_weights(w):
    n, d = w.shape[0], w.shape[1]
    wt = jnp.swapaxes(w, 1, 2).astype(BF16)
    dt0 = ZX_WIDTH
    u0 = dt0 + 2 * SSD_HEADS
    w_dt = wt[:, dt0:u0, :].reshape(n, 2, SSD_GROUPS, SSD_RANK, d)
    w_dt = jnp.transpose(w_dt, (0, 2, 1, 3, 4)).reshape(n, SSD_GROUPS, 2 * SSD_RANK, d)
    w_dt = jnp.pad(w_dt, ((0, 0), (0, 0), (0, LANES - 2 * SSD_RANK), (0, 0))).reshape(n, SSD_GROUPS * LANES, d)
    return wt, wt[:, u0:, :], w_dt


def _group_lanes(p):
    n = p.shape[0]
    p = jnp.transpose(p.reshape(n, 2, SSD_GROUPS, SSD_RANK), (0, 2, 1, 3)).reshape(n, SSD_GROUPS, 2 * SSD_RANK)
    return jnp.pad(p, ((0, 0), (0, 0), (0, LANES - 2 * SSD_RANK)))


def kernel(x, ffn1_norm, ffn1_w_gate, ffn1_w_up, ffn1_w_down, mix_norm, ffn2_norm, ffn2_w_gate, ffn2_w_up,
           ffn2_w_down, even_w_in, ssd_conv_w, ssd_conv_b, ssd_dt_bias, ssd_a_log, ssd_d, ssd_norm, fourier_w,
           even_w_out, sgu_w_uv, sgu_b_uv, sgu_norm, sgu_w_s, sgu_b_s, odd_w_out, final_norm):
    bsz, seq, d = x.shape
    assert bsz == 1 and d == D_MODEL
    depth = ffn1_norm.shape[0]
    n_even = even_w_in.shape[0]
    xs = x.reshape(seq, d)

    row3 = lambda a: a.reshape(a.shape[0], 1, a.shape[1])
    assert depth >= 1
    ffn_f32 = {1: (ffn1_w_gate, ffn1_w_up, ffn1_w_down), 2: (ffn2_w_gate, ffn2_w_up, ffn2_w_down)}
    ffn_nw = {1: row3(ffn1_norm), 2: row3(ffn2_norm)}
    ffn_w = tuple(w[0].astype(BF16) for w in ffn_f32[1])

    def ffn(xs, which, i, ffn_w):
        last = which == 2 and i == depth - 1
        nxt = None if last else ((*ffn_f32[2], i) if which == 1 else (*ffn_f32[1], i + 1))
        return _ffn(xs, ffn_nw[which], ffn_w, i, next_w=nxt, out_norm=final_norm.reshape(1, d) if last else None)

    mixn = row3(mix_norm)
    w_in, w_in_u, w_in_dt = _even_in_weights(even_w_in)
    zeros = jnp.zeros((n_even, SSD_GROUPS, SUBLANES - 2, LANES), F32)
    par = jnp.concatenate([_group_lanes(ssd_dt_bias)[:, :, None, :], _group_lanes(ssd_a_log)[:, :, None, :], zeros],
                          axis=2)
    dskip = jnp.repeat(ssd_d, SSD_HEAD_DIM, axis=1).reshape(n_even, SSD_GROUPS, 1, SSD_GW)
    ssd_nw = ssd_norm.reshape(n_even, SSD_GROUPS, 1, SSD_GW)
    conv_b = row3(ssd_conv_b)
    b_uv = row3(sgu_b_uv)
    sgu_nw = row3(sgu_norm)
    w_s = sgu_w_s.astype(BF16)
    b_s = jnp.broadcast_to(sgu_b_s[..., None], sgu_b_s.shape + (LANES,))

    w_uv = None
    for i in range(depth):
        xs, ffn_w = ffn(xs, 1, i, ffn_w)
        j = i // 2
        if i % 2 == 0:
            zx, u, dt_all, w_out = _inproj(xs, mixn, w_in, w_in_u, w_in_dt, (even_w_out, j), i, j)
            xbc = _conv(zx, ssd_conv_w, conv_b, j)
            y_ssd = _ssd(xbc, zx, dt_all, par, dskip, ssd_nw, j)
            fa, fb = _fourier_weights(fourier_w, j)
            y_fft = _dft_b(_dft_a(u), fa, fb, seq)
            nxt = (sgu_w_uv, j) if i + 1 < depth else None
            xs, w_uv = _outproj(xs, [y_ssd, y_fft], w_out, cast=nxt)
        else:
            gated, w_out = _sgu(xs, mixn, w_uv, b_uv, sgu_nw, w_s, b_s, (odd_w_out, j), i, j)
            xs, _ = _outproj(xs, [gated], w_out)
        xs, ffn_w = ffn(xs, 2, i, ffn_w)
    return xs.reshape(bsz, seq, d)
```

```python
import functools

import numpy as np
import jax
import jax.numpy as jnp
from jax import lax
from jax.experimental import pallas as pl
from jax.experimental.pallas import tpu as pltpu

F32 = jnp.float32
BF16 = jnp.bfloat16
HIGHEST = lax.Precision.HIGHEST
LOG2E = 1.4426950408889634

D_MODEL = 2048
D_FF = 5632
EPS = 1e-6
SSD_HEAD_DIM = 64
SSD_HEADS = 48
SSD_GROUPS = 8
SSD_RANK = SSD_HEADS // SSD_GROUPS
SSD_GW = SSD_RANK * SSD_HEAD_DIM
SSD_STATE = 128
SSD_CHUNK = 256
SSD_WIDTH = SSD_HEADS * SSD_HEAD_DIM
SSD_CONV = 5
SSD_CONV_CH = SSD_WIDTH + 2 * SSD_GROUPS * SSD_STATE
FOURIER_WIDTH = 1024
FOURIER_GROUPS = 4
FOURIER_GD = FOURIER_WIDTH // FOURIER_GROUPS
SGU_WIDTH = 4096
SGU_GROUPS = 8
SGU_GD = SGU_WIDTH // SGU_GROUPS
SGU_CHUNK = 128
ZX_WIDTH = SSD_WIDTH + SSD_CONV_CH

LANES = 128
SUBLANES = 8
VMEM_LIMIT = 56 * 1024 * 1024
VMEM_LIMIT_BIG = 60 * 1024 * 1024

TM = 512
TMX = 1024
TF = 512
TN = 1024
SSD_GPS = 2
DFT_N1 = 128
DFT_KB = 8


def _cparams(*sem, vmem_limit=VMEM_LIMIT):
    return pltpu.CompilerParams(dimension_semantics=sem, vmem_limit_bytes=vmem_limit)


def _rmsnorm(x, w):
    ms = jnp.mean(x * x, axis=-1, keepdims=True)
    return x * lax.rsqrt(ms + EPS) * w


def _ffn_body(*refs, cast_next, out_norm):
    x_ref, nw_ref, wg_ref, wu_ref, wd_ref = refs[:5]
    pos = 5
    nxt_in = refs[pos:pos + 3] if cast_next else ()
    pos += len(nxt_in)
    out_nw_ref = refs[pos] if out_norm else None
    pos += int(out_norm)
    o_ref = refs[pos]
    nxt_out = refs[pos + 1:pos + 1 + len(nxt_in)]
    xn_ref = refs[-1]
    j = pl.program_id(1)

    def half_swiglu():
        xn = xn_ref[...]
        g = jnp.dot(xn, wg_ref[...], preferred_element_type=F32)
        u = jnp.dot(xn, wu_ref[...], preferred_element_type=F32)
        h = ((0.5 * g) * jax.nn.sigmoid(g) * u).astype(BF16)
        return jnp.dot(h, wd_ref[...], preferred_element_type=F32)

    def cast_weight_blocks():
        for src, dst in zip(nxt_in, nxt_out):
            dst[...] = src[...].astype(BF16)

    @pl.when(j == 0)
    def _():
        xn_ref[...] = _rmsnorm(x_ref[...], nw_ref[...]).astype(BF16)
        o_ref[...] = x_ref[...] + half_swiglu()
        cast_weight_blocks()

    @pl.when(j > 0)
    def _():
        o_ref[...] += half_swiglu()
        cast_weight_blocks()

    if out_norm:
        @pl.when(j == pl.num_programs(1) - 1)
        def _():
            o_ref[...] = _rmsnorm(o_ref[...], out_nw_ref[...])


def _ffn(x, nw, w, layer, next_w=None, out_norm=None):
    seq = x.shape[0]
    tm = min(TMX, seq)
    ni, nj = seq // tm, D_FF // TF
    in_specs = [
        pl.BlockSpec((tm, D_MODEL), lambda i, j: (i, 0)),
        pl.BlockSpec((None, 1, D_MODEL), lambda i, j: (layer, 0, 0)),
        pl.BlockSpec((D_MODEL, TF), lambda i, j: (0, j)),
        pl.BlockSpec((D_MODEL, TF), lambda i, j: (0, j)),
        pl.BlockSpec((TF, D_MODEL), lambda i, j: (j, 0)),
    ]
    operands = [x, nw, *w]
    out_specs = [pl.BlockSpec((tm, D_MODEL), lambda i, j: (i, 0))]
    out_shape = [jax.ShapeDtypeStruct((seq, D_MODEL), F32)]
    if next_w is not None:
        g32, u32, d32, nl = next_w
        assert D_MODEL % ni == 0
        rb = D_MODEL // ni
        in_specs += [pl.BlockSpec((None, rb, TF), lambda i, j: (nl, i, j)),
                     pl.BlockSpec((None, rb, TF), lambda i, j: (nl, i, j)),
                     pl.BlockSpec((None, TF, rb), lambda i, j: (nl, j, i))]
        operands += [g32, u32, d32]
        out_specs += [pl.BlockSpec((rb, TF), lambda i, j: (i, j)),
                      pl.BlockSpec((rb, TF), lambda i, j: (i, j)),
                      pl.BlockSpec((TF, rb), lambda i, j: (j, i))]
        out_shape += [jax.ShapeDtypeStruct((D_MODEL, D_FF), BF16),
                      jax.ShapeDtypeStruct((D_MODEL, D_FF), BF16),
                      jax.ShapeDtypeStruct((D_FF, D_MODEL), BF16)]
    if out_norm is not None:
        in_specs.append(pl.BlockSpec((1, D_MODEL), lambda i, j: (0, 0)))
        operands.append(out_norm)
    outs = pl.pallas_call(
        functools.partial(_ffn_body, cast_next=next_w is not None, out_norm=out_norm is not None),
        grid=(ni, nj),
        in_specs=in_specs,
        out_specs=out_specs,
        out_shape=out_shape,
        scratch_shapes=[pltpu.VMEM((tm, D_MODEL), BF16)],
        compiler_params=_cparams("parallel", "arbitrary", vmem_limit=VMEM_LIMIT_BIG),
        name="ffn",
    )(*operands)
    return outs[0], tuple(outs[1:])


N_ZX_TILES = ZX_WIDTH // TN
IN_TILES = N_ZX_TILES + 2


def _inproj_body(x_ref, nw_ref, w_ref, wu_ref, wdt_ref, cast_src, zx_ref, u_ref, dt_ref, cast_dst, h_ref):
    j = pl.program_id(1)
    cast_dst[...] = cast_src[...].astype(BF16)

    @pl.when(j == 0)
    def _():
        h_ref[...] = _rmsnorm(x_ref[...], nw_ref[...]).astype(BF16)

    @pl.when(j < N_ZX_TILES)
    def _():
        zx_ref[...] = lax.dot_general(h_ref[...], w_ref[...], (((1,), (1,)), ((), ())),
                                      preferred_element_type=F32).astype(BF16)

    @pl.when(j == N_ZX_TILES)
    def _():
        u_ref[...] = lax.dot_general(h_ref[...], wu_ref[...], (((1,), (1,)), ((), ())),
                                     preferred_element_type=F32).astype(BF16)

    @pl.when(j == N_ZX_TILES + 1)
    def _():
        dt_ref[...] = lax.dot_general(h_ref[...], wdt_ref[...], (((1,), (1,)), ((), ())),
                                      preferred_element_type=F32)


def _inproj(x, nw, w, w_u, w_dt, cast, layer, j_even):
    seq = x.shape[0]
    tm = min(TMX, seq)
    ni = seq // tm
    once = pl.Buffered(1)
    c_in, c_out, c_shape = _cast_specs(cast, ni, N_ZX_TILES, lambda i, j: (i, jnp.minimum(j, N_ZX_TILES - 1)))
    return pl.pallas_call(
        _inproj_body,
        grid=(ni, IN_TILES),
        in_specs=[
            pl.BlockSpec((tm, D_MODEL), lambda i, j: (i, 0)),
            pl.BlockSpec((None, 1, D_MODEL), lambda i, j: (layer, 0, 0)),
            pl.BlockSpec((None, TN, D_MODEL), lambda i, j: (j_even, jnp.minimum(j, N_ZX_TILES - 1), 0)),
            pl.BlockSpec((None, FOURIER_WIDTH, D_MODEL), lambda i, j: (j_even, 0, 0), pipeline_mode=once),
            pl.BlockSpec((None, SSD_GROUPS * LANES, D_MODEL), lambda i, j: (j_even, 0, 0), pipeline_mode=once),
            c_in,
        ],
        out_specs=[
            pl.BlockSpec((tm, TN), lambda i, j: (i, jnp.minimum(j, N_ZX_TILES - 1))),
            pl.BlockSpec((tm, FOURIER_WIDTH), lambda i, j: (i, 0)),
            pl.BlockSpec((tm, SSD_GROUPS * LANES), lambda i, j: (i, 0)),
            c_out,
        ],
        out_shape=[
            jax.ShapeDtypeStruct((seq, ZX_WIDTH), BF16),
            jax.ShapeDtypeStruct((seq, FOURIER_WIDTH), BF16),
            jax.ShapeDtypeStruct((seq, SSD_GROUPS * LANES), F32),
            c_shape,
        ],
        scratch_shapes=[pltpu.VMEM((tm, D_MODEL), BF16)],
        compiler_params=_cparams("parallel", "arbitrary", vmem_limit=VMEM_LIMIT_BIG),
        name="even_inproj",
    )(x, nw, w, w_u, w_dt, cast[0])


CONV_TR = 2048
CONV_TC = 512
CONV_HALO = 16


def _conv_body(xm_ref, xp_ref, xn_ref, w_ref, b_ref, o_ref):
    i = pl.program_id(0)
    last = pl.num_programs(0) - 1
    tr = xm_ref.shape[0]
    prev = xp_ref[...].astype(F32)[CONV_HALO - SUBLANES:]
    nxt = xn_ref[...].astype(F32)[:SUBLANES]
    ext = jnp.concatenate([jnp.where(i == 0, 0.0, prev), xm_ref[...].astype(F32), jnp.where(i == last, 0.0, nxt)],
                          axis=0)
    n = tr + 2 * SUBLANES
    w = w_ref[...]
    acc = jnp.broadcast_to(b_ref[...], o_ref.shape)
    half = SSD_CONV // 2
    for k in range(SSD_CONV):
        shifted = ext if k == half else pltpu.roll(ext, (half - k) % n, axis=0)
        acc = acc + shifted[SUBLANES:SUBLANES + tr, :] * w[k:k + 1, :]
    o_ref[...] = (acc * jax.nn.sigmoid(acc)).astype(BF16)


def _conv(zx, conv_w, conv_b, layer):
    seq = zx.shape[0]
    tr = min(CONV_TR, seq)
    col0 = SSD_WIDTH // CONV_TC
    hb = tr // CONV_HALO
    nhb = seq // CONV_HALO
    return pl.pallas_call(
        _conv_body,
        grid=(seq // tr, SSD_CONV_CH // CONV_TC),
        in_specs=[
            pl.BlockSpec((tr, CONV_TC), lambda i, j: (i, col0 + j)),
            pl.BlockSpec((CONV_HALO, CONV_TC), lambda i, j: (jnp.maximum(i * hb - 1, 0), col0 + j)),
            pl.BlockSpec((CONV_HALO, CONV_TC), lambda i, j: (jnp.minimum((i + 1) * hb, nhb - 1), col0 + j)),
            pl.BlockSpec((None, SSD_CONV, CONV_TC), lambda i, j: (layer, 0, j)),
            pl.BlockSpec((None, 1, CONV_TC), lambda i, j: (layer, 0, j)),
        ],
        out_specs=pl.BlockSpec((tr, CONV_TC), lambda i, j: (i, j)),
        out_shape=jax.ShapeDtypeStruct((seq, SSD_CONV_CH), BF16),
        compiler_params=_cparams("parallel", "parallel"),
        name="ssd_conv",
    )(zx, zx, zx, conv_w, conv_b)


def _softplus(v):
    return jnp.maximum(v, 0.0) + jnp.log1p(jnp.exp(-jnp.abs(v)))


def _head_rows(rows, first):
    n = rows.shape[1]
    return jnp.concatenate([jnp.broadcast_to(rows[first + h:first + h + 1, :], (SSD_HEAD_DIM, n))
                            for h in range(SSD_RANK)], axis=0)


def _ssd_body(x_ref, b_ref, c_ref, z_ref, dt_ref, par_ref, dskip_ref, nw_ref, tri_ref, o_ref, *scratch):
    phase = pl.program_id(1)
    c = pl.program_id(2)
    nc = pl.num_programs(2)
    sf_ref, sb_ref = scratch[:2]

    def group(gi):
        cols = lambda w: slice(gi * w, (gi + 1) * w)
        return dict(x_ref=x_ref.at[:, cols(SSD_GW)], b_ref=b_ref.at[:, cols(SSD_STATE)],
                    c_ref=c_ref.at[:, cols(SSD_STATE)], z_ref=z_ref.at[:, cols(SSD_GW)],
                    dt_ref=dt_ref.at[:, cols(LANES)], par_ref=par_ref.at[gi], dskip_ref=dskip_ref.at[gi],
                    nw_ref=nw_ref.at[gi], tri_ref=tri_ref, o_ref=o_ref.at[:, cols(SSD_GW)],
                    scratch=[s.at[gi] for s in scratch])

    @pl.when(phase == 0)
    def _():
        @pl.when(c == 0)
        def _():
            sb_ref[...] = jnp.zeros_like(sb_ref)

        for gi in range(SSD_GPS):
            _ssd_prepare(nc - 1 - c, **group(gi))

    @pl.when(phase == 1)
    def _():
        @pl.when(c == 0)
        def _():
            sf_ref[...] = jnp.zeros_like(sf_ref)

        for gi in range(SSD_GPS):
            _ssd_emit(c, **group(gi))


def _ssd_prepare(cc, x_ref, b_ref, dt_ref, par_ref, tri_ref, scratch, **unused):
    _, sb_ref, sball_ref, xt_ref, qn_ref, dtt_ref, qt_ref, tott_ref = scratch
    t = SSD_CHUNK
    rk = SSD_RANK
    prm = dtt_ref.shape[1]
    bm = b_ref[...]
    par = par_ref[...]
    lane = lax.broadcasted_iota(jnp.int32, (1, LANES), 1)
    dt = _softplus(dt_ref[...] + par[0:1, :])
    da = dt * (-jnp.exp(par[1:2, :]))
    d1 = da.astype(BF16)
    r1 = da - d1.astype(F32)
    d2 = r1.astype(BF16)
    d3 = (r1 - d2.astype(F32)).astype(BF16)
    cs3 = jnp.dot(tri_ref[...], jnp.concatenate([d1, d2, d3], axis=1), preferred_element_type=F32)
    cs = cs3[:, :LANES] + cs3[:, LANES:2 * LANES] + cs3[:, 2 * LANES:]
    q = (cs - jnp.where(lane >= rk, da, 0.0)) * LOG2E
    dt_t = dt.T[:prm]
    q_t = q.T[:prm]
    tot_t = jnp.broadcast_to((cs * LOG2E).T[:prm, t - 1:t], (prm, t))
    x_t = x_ref[...].astype(F32).T
    xt_ref[cc] = x_t
    qn_ref[cc] = q
    dtt_ref[cc] = dt_t
    qt_ref[cc] = q_t
    tott_ref[cc] = tot_t

    sb = sb_ref[...]
    sball_ref[cc] = sb.astype(BF16)
    wb = dt_t * jnp.exp2(q_t)
    xw = (x_t * _head_rows(wb, rk)).astype(BF16)
    upd = jnp.dot(xw, bm, preferred_element_type=F32)
    sb_ref[...] = _head_rows(jnp.exp2(tot_t[:, :SSD_STATE]), rk) * sb + upd


def _ssd_emit(c, x_ref, b_ref, c_ref, z_ref, dskip_ref, nw_ref, o_ref, scratch, **unused):
    sf_ref, _, sball_ref, xt_ref, qn_ref, dtt_ref, qt_ref, tott_ref = scratch
    t = SSD_CHUNK
    rk = SSD_RANK
    hd = SSD_HEAD_DIM
    bm = b_ref[...]
    x_t = xt_ref[c]
    q = qn_ref[c]
    dt_t = dtt_ref[c]
    q_t = qt_ref[c]
    tot_t = tott_ref[c]
    cm = c_ref[...]
    gt = lax.dot_general(bm, cm, (((1,), (1,)), ((), ())), preferred_element_type=F32)
    hf = t // 2
    srow = lax.broadcasted_iota(jnp.int32, (hf, hf), 0)
    tcol = lax.broadcasted_iota(jnp.int32, (hf, hf), 1)
    causal = srow <= tcol
    anti = srow >= tcol
    zero = jnp.zeros((hf, hf), BF16)
    g00, g01, g10, g11 = gt[:hf, :hf], gt[:hf, hf:], gt[hf:, :hf], gt[hf:, hf:]
    ys = []
    for h in range(rk):
        hb = rk + h
        xh = x_t[h * hd:(h + 1) * hd, :]
        lhs = jnp.concatenate([(xh * dt_t[h:h + 1, :]).astype(BF16), (xh * dt_t[hb:hb + 1, :]).astype(BF16)],
                              axis=1)
        qs, qt = q[:, h:h + 1], q_t[h:h + 1, :]
        f00 = (g00 * jnp.where(causal, jnp.exp2(qt[:, :hf] - qs[:hf]), 0.0)).astype(BF16)
        f01 = (g01 * jnp.exp2(qt[:, hf:] - qs[:hf])).astype(BF16)
        f11 = (g11 * jnp.where(causal, jnp.exp2(qt[:, hf:] - qs[hf:]), 0.0)).astype(BF16)
        qs, qt = q[:, hb:hb + 1], q_t[hb:hb + 1, :]
        b00 = (g00 * jnp.where(anti, jnp.exp2(qs[:hf] - qt[:, :hf]), 0.0)).astype(BF16)
        b10 = (g10 * jnp.exp2(qs[hf:] - qt[:, :hf])).astype(BF16)
        b11 = (g11 * jnp.where(anti, jnp.exp2(qs[hf:] - qt[:, hf:]), 0.0)).astype(BF16)
        rhs = jnp.concatenate([jnp.concatenate([f00, f01], axis=1), jnp.concatenate([zero, f11], axis=1),
                               jnp.concatenate([b00, zero], axis=1), jnp.concatenate([b10, b11], axis=1)],
                              axis=0)
        ys.append(jnp.dot(lhs, rhs, preferred_element_type=F32))
    y_t = jnp.concatenate(ys, axis=0)

    sf = sf_ref[...]
    states = jnp.concatenate([sf.astype(BF16), sball_ref[c]], axis=0)
    off = lax.dot_general(states, cm, (((1,), (1,)), ((), ())), preferred_element_type=F32)
    y_t += off[:SSD_GW] * _head_rows(jnp.exp2(q_t), 0)
    y_t += off[SSD_GW:] * _head_rows(jnp.exp2(tot_t - q_t), rk)
    wf = dt_t * jnp.exp2(tot_t - q_t)
    xw = (x_t * _head_rows(wf, 0)).astype(BF16)
    upd = jnp.dot(xw, bm, preferred_element_type=F32)
    sf_ref[...] = _head_rows(jnp.exp2(tot_t[:, :SSD_STATE]), 0) * sf + upd

    y = y_t.T + dskip_ref[...] * x_ref[...].astype(F32)
    z = z_ref[...].astype(F32)
    y = y * (z * jax.nn.sigmoid(z))
    o_ref[...] = _rmsnorm(y, nw_ref[...]).astype(BF16)


def _ssd(xbc, zx, dt_all, par, dskip, nw, layer):
    seq = xbc.shape[0]
    nc = seq // SSD_CHUNK
    t = SSD_CHUNK
    b0 = SSD_WIDTH // SSD_STATE
    c0 = b0 + SSD_GROUPS
    tri = jnp.asarray(np.tril(np.ones((t, t), np.float32)), dtype=BF16)

    def cidx(p, c):
        return p * c + (1 - p) * (nc - 1 - c)

    gps = SSD_GPS
    assert b0 % gps == 0 and c0 % gps == 0
    return pl.pallas_call(
        _ssd_body,
        grid=(SSD_GROUPS // gps, 2, nc),
        in_specs=[
            pl.BlockSpec((t, gps * SSD_GW), lambda g, p, c: (cidx(p, c), g)),
            pl.BlockSpec((t, gps * SSD_STATE), lambda g, p, c: (cidx(p, c), b0 // gps + g)),
            pl.BlockSpec((t, gps * SSD_STATE), lambda g, p, c: (p * c, c0 // gps + g)),
            pl.BlockSpec((t, gps * SSD_GW), lambda g, p, c: (p * c, g)),
            pl.BlockSpec((t, gps * LANES), lambda g, p, c: ((1 - p) * (nc - 1 - c), g)),
            pl.BlockSpec((None, gps, SUBLANES, LANES), lambda g, p, c: (layer, g, 0, 0)),
            pl.BlockSpec((None, gps, 1, SSD_GW), lambda g, p, c: (layer, g, 0, 0)),
            pl.BlockSpec((None, gps, 1, SSD_GW), lambda g, p, c: (layer, g, 0, 0)),
            pl.BlockSpec((t, t), lambda g, p, c: (0, 0)),
        ],
        out_specs=pl.BlockSpec((t, gps * SSD_GW), lambda g, p, c: (p * c, g)),
        out_shape=jax.ShapeDtypeStruct((seq, SSD_WIDTH), BF16),
        scratch_shapes=[
            pltpu.VMEM((gps, SSD_GW, SSD_STATE), F32),
            pltpu.VMEM((gps, SSD_GW, SSD_STATE), F32),
            pltpu.VMEM((gps, nc, SSD_GW, SSD_STATE), BF16),
            pltpu.VMEM((gps, nc, SSD_GW, t), F32),
            pltpu.VMEM((gps, nc, t, LANES), F32),
            pltpu.VMEM((gps, nc, 2 * SUBLANES, t), F32),
            pltpu.VMEM((gps, nc, 2 * SUBLANES, t), F32),
            pltpu.VMEM((gps, nc, 2 * SUBLANES, t), F32),
        ],
        compiler_params=_cparams("arbitrary", "arbitrary", "arbitrary"),
        name="ssd_scan",
    )(xbc, xbc, xbc, zx, dt_all, par, dskip, nw, tri)


def _fw_body(cd_ref, sd_ref, w_ref, a_ref, b_ref):
    w = w_ref[...]
    a_ref[...] = jnp.dot(cd_ref[...], w, preferred_element_type=F32, precision=HIGHEST).astype(BF16)
    b_ref[...] = jnp.dot(sd_ref[...], w, preferred_element_type=F32, precision=HIGHEST).astype(BF16)


def _fourier_weights(fourier_w, layer):
    d = FOURIER_GD
    ang = 2.0 * np.pi * np.outer(np.arange(d), np.arange(d)) / d
    cd = jnp.asarray((np.cos(ang) / np.sqrt(d)).astype(np.float32))
    sd = jnp.asarray((np.sin(ang) / np.sqrt(d)).astype(np.float32))
    return pl.pallas_call(
        _fw_body,
        grid=(FOURIER_GROUPS,),
        in_specs=[
            pl.BlockSpec((d, d), lambda g: (0, 0)),
            pl.BlockSpec((d, d), lambda g: (0, 0)),
            pl.BlockSpec((None, None, d, d), lambda g: (layer, g, 0, 0)),
        ],
        out_specs=[pl.BlockSpec((None, d, d), lambda g: (g, 0, 0))] * 2,
        out_shape=[jax.ShapeDtypeStruct((FOURIER_GROUPS, d, d), BF16)] * 2,
        compiler_params=_cparams("parallel"),
        name="fourier_weights",
    )(cd, sd, fourier_w)


DFT_NB = 4


def _dft_a_body(x_ref, f_ref, tc_ref, ts_ref, o_ref):
    n1 = DFT_N1
    y = jnp.dot(f_ref[...], x_ref[...], preferred_element_type=F32)
    reps = FOURIER_WIDTH // LANES
    for b in range(tc_ref.shape[0]):
        sl = slice(b * FOURIER_WIDTH, (b + 1) * FOURIER_WIDTH)
        yr = y[:n1, sl]
        yi = y[n1:, sl]
        tc = jnp.tile(tc_ref[b], (1, reps))
        ts = jnp.tile(ts_ref[b], (1, reps))
        o_ref[:n1, sl] = (yr * tc + yi * ts).astype(BF16)
        o_ref[n1:, sl] = (yi * tc - yr * ts).astype(BF16)


def _dft_a(u):
    seq = u.shape[0]
    n1 = DFT_N1
    n2 = seq // n1
    nb = min(DFT_NB, n2)
    ang1 = 2.0 * np.pi * np.outer(np.arange(n1), np.arange(n1)) / n1
    f1 = jnp.asarray(np.concatenate([np.cos(ang1), -np.sin(ang1)], axis=0), dtype=BF16)
    angt = 2.0 * np.pi * np.outer(np.arange(n2), np.arange(n1)) / seq
    tc = jnp.asarray(np.repeat(np.cos(angt)[:, :, None], LANES, axis=2).astype(np.float32))
    ts = jnp.asarray(np.repeat(np.sin(angt)[:, :, None], LANES, axis=2).astype(np.float32))
    x2 = u.reshape(n1, n2 * FOURIER_WIDTH)
    return pl.pallas_call(
        _dft_a_body,
        grid=(n2 // nb,),
        in_specs=[
            pl.BlockSpec((n1, nb * FOURIER_WIDTH), lambda j: (0, j)),
            pl.BlockSpec((2 * n1, n1), lambda j: (0, 0)),
            pl.BlockSpec((nb, n1, LANES), lambda j: (j, 0, 0)),
            pl.BlockSpec((nb, n1, LANES), lambda j: (j, 0, 0)),
        ],
        out_specs=pl.BlockSpec((2 * n1, nb * FOURIER_WIDTH), lambda j: (0, j)),
        out_shape=jax.ShapeDtypeStruct((2 * n1, n2 * FOURIER_WIDTH), BF16),
        compiler_params=_cparams("parallel"),
        name="dft_stage_a",
    )(x2, f1, tc, ts)


def _dft_b_body(yr_ref, yi_ref, lr_ref, li_ref, a_ref, b_ref, o_ref):
    kb, n2, width = yr_ref.shape
    rhs = jnp.concatenate([yr_ref[...].reshape(kb * n2, width), yi_ref[...].reshape(kb * n2, width)], axis=0)
    zr = jnp.dot(lr_ref[...], rhs, preferred_element_type=F32).astype(BF16)
    zi = jnp.dot(li_ref[...], rhs, preferred_element_type=F32).astype(BF16)
    outs = []
    for g in range(FOURIER_GROUPS):
        sl = slice(g * FOURIER_GD, (g + 1) * FOURIER_GD)
        outs.append(jnp.dot(zr[:, sl], a_ref[g], preferred_element_type=F32)
                    + jnp.dot(zi[:, sl], b_ref[g], preferred_element_type=F32))
    o_ref[...] = jnp.concatenate(outs, axis=1).reshape(o_ref.shape)


def _dft_b(ya, fa, fb, seq):
    n1 = DFT_N1
    n2 = seq // n1
    kb = DFT_KB
    ang2 = 2.0 * np.pi * np.outer(np.arange(n2), np.arange(n2)) / n2
    c2 = np.cos(ang2) / np.sqrt(seq)
    s2 = np.sin(ang2) / np.sqrt(seq)
    eye = np.eye(kb)
    lr = np.concatenate([np.einsum('ab,kn->kabn', eye, c2).reshape(n2 * kb, kb * n2),
                         np.einsum('ab,kn->kabn', eye, s2).reshape(n2 * kb, kb * n2)], axis=1)
    li = np.concatenate([np.einsum('ab,kn->kabn', eye, -s2).reshape(n2 * kb, kb * n2),
                         np.einsum('ab,kn->kabn', eye, c2).reshape(n2 * kb, kb * n2)], axis=1)
    y3 = ya.reshape(2 * n1, n2, FOURIER_WIDTH)
    nk = n1 // kb
    out = pl.pallas_call(
        _dft_b_body,
        grid=(nk,),
        in_specs=[
            pl.BlockSpec((kb, n2, FOURIER_WIDTH), lambda j: (j, 0, 0)),
            pl.BlockSpec((kb, n2, FOURIER_WIDTH), lambda j: (nk + j, 0, 0)),
            pl.BlockSpec((n2 * kb, 2 * kb * n2), lambda j: (0, 0)),
            pl.BlockSpec((n2 * kb, 2 * kb * n2), lambda j: (0, 0)),
            pl.BlockSpec((FOURIER_GROUPS, FOURIER_GD, FOURIER_GD), lambda j: (0, 0, 0)),
            pl.BlockSpec((FOURIER_GROUPS, FOURIER_GD, FOURIER_GD), lambda j: (0, 0, 0)),
        ],
        out_specs=pl.BlockSpec((n2, kb, FOURIER_WIDTH), lambda j: (0, j, 0)),
        out_shape=jax.ShapeDtypeStruct((n2, n1, FOURIER_WIDTH), F32),
        compiler_params=_cparams("parallel"),
        name="dft_stage_b",
    )(y3, y3, jnp.asarray(lr, dtype=BF16), jnp.asarray(li, dtype=BF16), fa, fb)
    return out.reshape(seq, FOURIER_WIDTH)


def _cast_specs(cast, n_rows, n_cols, block_index):
    src, layer = cast
    _, r, c = src.shape
    assert r % n_rows == 0 and c % n_cols == 0
    blk = (r // n_rows, c // n_cols)
    in_spec = pl.BlockSpec((None,) + blk, lambda *g: (layer,) + tuple(block_index(*g)))
    out_spec = pl.BlockSpec(blk, lambda *g: tuple(block_index(*g)))
    return in_spec, out_spec, jax.ShapeDtypeStruct((r, c), BF16)


def _outproj_body(*refs, n_lhs, cast):
    x_ref = refs[0]
    lhs = refs[1:1 + n_lhs]
    ws = refs[1 + n_lhs:1 + 2 * n_lhs]
    o_ref = refs[1 + 2 * n_lhs + int(cast)]
    acc = x_ref[...]
    for a_ref, w_ref in zip(lhs, ws):
        acc = acc + jnp.dot(a_ref[...].astype(BF16), w_ref[...], preferred_element_type=F32)
    o_ref[...] = acc
    if cast:
        refs[-1][...] = refs[1 + 2 * n_lhs][...].astype(BF16)


def _outproj(x, lhs_list, w, cast=None):
    seq = x.shape[0]
    nj, ni = D_MODEL // TN, seq // TM
    in_specs = [pl.BlockSpec((TM, TN), lambda j, i: (i, j))]
    for a in lhs_list:
        in_specs.append(pl.BlockSpec((TM, a.shape[1]), lambda j, i: (i, 0)))
    row = 0
    for a in lhs_list:
        k = a.shape[1]
        assert row % k == 0
        in_specs.append(pl.BlockSpec((k, TN), lambda j, i, rb=row // k: (rb, j)))
        row += k
    operands = [x, *lhs_list, *([w] * len(lhs_list))]
    out_specs = [pl.BlockSpec((TM, TN), lambda j, i: (i, j))]
    out_shape = [jax.ShapeDtypeStruct((seq, D_MODEL), F32)]
    if cast is not None:
        c_in, c_out, c_shape = _cast_specs(cast, ni, nj, lambda j, i: (i, j))
        in_specs.append(c_in)
        operands.append(cast[0])
        out_specs.append(c_out)
        out_shape.append(c_shape)
    outs = pl.pallas_call(
        functools.partial(_outproj_body, n_lhs=len(lhs_list), cast=cast is not None),
        grid=(nj, ni),
        in_specs=in_specs,
        out_specs=out_specs,
        out_shape=out_shape,
        compiler_params=_cparams("parallel", "parallel"),
        name="outproj",
    )(*operands)
    return outs[0], (outs[1] if cast is not None else None)


SGU_V_TILES = SGU_WIDTH // TN
SGU_TILES = 2 * SGU_V_TILES


def _sgu_body(x_ref, nw_ref, w_ref, b_ref, vnw_ref, ws_ref, bs_ref, cast_src, o_ref, cast_dst, h_ref, v_ref, ss_ref):
    j = pl.program_id(1)
    tm = x_ref.shape[0]
    cast_dst[...] = cast_src[...].astype(BF16)

    @pl.when(j == 0)
    def _():
        h_ref[...] = _rmsnorm(x_ref[...], nw_ref[...]).astype(BF16)
        ss_ref[...] = jnp.zeros_like(ss_ref)

    def project():
        return jax.nn.gelu(jnp.dot(h_ref[...], w_ref[...], preferred_element_type=F32) + b_ref[...])

    @pl.when(j < SGU_V_TILES)
    def _():
        act = project()
        v_ref[j] = act.astype(BF16)
        ss_ref[...] += jnp.sum(act * act, axis=-1, keepdims=True)

    @pl.when(j == SGU_V_TILES - 1)
    def _():
        rs = lax.rsqrt(ss_ref[...] * (1.0 / SGU_WIDTH) + EPS)
        per_tile = TN // SGU_GD
        for g in range(SGU_GROUPS):
            tile, off = divmod(g, per_tile)
            sl = slice(off * SGU_GD, (off + 1) * SGU_GD)
            v = v_ref[tile, :, sl].astype(F32)
            v = (v * rs * vnw_ref[:, g * SGU_GD:(g + 1) * SGU_GD]).astype(BF16)
            bias = jnp.tile(bs_ref[g], (1, SGU_GD // LANES))
            for qc in range(tm // SGU_CHUNK):
                rows = slice(qc * SGU_CHUNK, (qc + 1) * SGU_CHUNK)
                mixed = jnp.dot(ws_ref[g], v[rows], preferred_element_type=F32) + bias
                v_ref[tile, rows, sl] = mixed.astype(BF16)

    @pl.when(j >= SGU_V_TILES)
    def _():
        o_ref[...] = (project() * v_ref[j - SGU_V_TILES].astype(F32)).astype(BF16)


def _sgu(x, nw, w_uv, b_uv, vnw, w_s, b_s, cast, layer, j_odd):
    seq = x.shape[0]
    tm = min(TMX, seq)
    ni = seq // tm
    wcol = lambda j: (j + SGU_V_TILES) % SGU_TILES
    c_in, c_out, c_shape = _cast_specs(cast, ni, SGU_TILES, lambda i, j: (i, j))
    return pl.pallas_call(
        _sgu_body,
        grid=(ni, SGU_TILES),
        in_specs=[
            pl.BlockSpec((tm, D_MODEL), lambda i, j: (i, 0)),
            pl.BlockSpec((None, 1, D_MODEL), lambda i, j: (layer, 0, 0)),
            pl.BlockSpec((D_MODEL, TN), lambda i, j: (0, wcol(j))),
            pl.BlockSpec((None, 1, TN), lambda i, j: (j_odd, 0, wcol(j))),
            pl.BlockSpec((None, 1, SGU_WIDTH), lambda i, j: (j_odd, 0, 0)),
            pl.BlockSpec((None, SGU_GROUPS, SGU_CHUNK, SGU_CHUNK), lambda i, j: (j_odd, 0, 0, 0)),
            pl.BlockSpec((None, SGU_GROUPS, SGU_CHUNK, LANES), lambda i, j: (j_odd, 0, 0, 0)),
            c_in,
        ],
        out_specs=[pl.BlockSpec((tm, TN), lambda i, j: (i, jnp.maximum(j - SGU_V_TILES, 0))), c_out],
        out_shape=[jax.ShapeDtypeStruct((seq, SGU_WIDTH), BF16), c_shape],
        scratch_shapes=[
            pltpu.VMEM((tm, D_MODEL), BF16),
            pltpu.VMEM((SGU_V_TILES, tm, TN), BF16),
            pltpu.VMEM((tm, 1), F32),
        ],
        compiler_params=_cparams("parallel", "arbitrary"),
        name="sgu",
    )(x, nw, w_uv, b_uv, vnw, w_s, b_s, cast[0])


def _even_in_weights(w):
    n, d = w.shape[0], w.shape[1]
    wt = jnp.swapaxes(w, 1, 2).astype(BF16)
    dt0 = ZX_WIDTH
    u0 = dt0 + 2 * SSD_HEADS
    w_dt = wt[:, dt0:u0, :].reshape(n, 2, SSD_GROUPS, SSD_RANK, d)
    w_dt = jnp.transpose(w_dt, (0, 2, 1, 3, 4)).reshape(n, SSD_GROUPS, 2 * SSD_RANK, d)
    w_dt = jnp.pad(w_dt, ((0, 0), (0, 0), (0, LANES - 2 * SSD_RANK), (0, 0))).reshape(n, SSD_GROUPS * LANES, d)
    return wt, wt[:, u0:, :], w_dt


def _group_lanes(p):
    n = p.shape[0]
    p = jnp.transpose(p.reshape(n, 2, SSD_GROUPS, SSD_RANK), (0, 2, 1, 3)).reshape(n, SSD_GROUPS, 2 * SSD_RANK)
    return jnp.pad(p, ((0, 0), (0, 0), (0, LANES - 2 * SSD_RANK)))


def kernel(x, ffn1_norm, ffn1_w_gate, ffn1_w_up, ffn1_w_down, mix_norm, ffn2_norm, ffn2_w_gate, ffn2_w_up,
           ffn2_w_down, even_w_in, ssd_conv_w, ssd_conv_b, ssd_dt_bias, ssd_a_log, ssd_d, ssd_norm, fourier_w,
           even_w_out, sgu_w_uv, sgu_b_uv, sgu_norm, sgu_w_s, sgu_b_s, odd_w_out, final_norm):
    bsz, seq, d = x.shape
    assert bsz == 1 and d == D_MODEL
    depth = ffn1_norm.shape[0]
    n_even = even_w_in.shape[0]
    xs = x.reshape(seq, d)

    row3 = lambda a: a.reshape(a.shape[0], 1, a.shape[1])
    assert depth >= 1
    ffn_f32 = {1: (ffn1_w_gate, ffn1_w_up, ffn1_w_down), 2: (ffn2_w_gate, ffn2_w_up, ffn2_w_down)}
    ffn_nw = {1: row3(ffn1_norm), 2: row3(ffn2_norm)}
    ffn_w = tuple(w[0].astype(BF16) for w in ffn_f32[1])

    def ffn(xs, which, i, ffn_w):
        last = which == 2 and i == depth - 1
        nxt = None if last else ((*ffn_f32[2], i) if which == 1 else (*ffn_f32[1], i + 1))
        return _ffn(xs, ffn_nw[which], ffn_w, i, next_w=nxt, out_norm=final_norm.reshape(1, d) if last else None)

    mixn = row3(mix_norm)
    w_in, w_in_u, w_in_dt = _even_in_weights(even_w_in)
    zeros = jnp.zeros((n_even, SSD_GROUPS, SUBLANES - 2, LANES), F32)
    par = jnp.concatenate([_group_lanes(ssd_dt_bias)[:, :, None, :], _group_lanes(ssd_a_log)[:, :, None, :], zeros],
                          axis=2)
    dskip = jnp.repeat(ssd_d, SSD_HEAD_DIM, axis=1).reshape(n_even, SSD_GROUPS, 1, SSD_GW)
    ssd_nw = ssd_norm.reshape(n_even, SSD_GROUPS, 1, SSD_GW)
    conv_b = row3(ssd_conv_b)
    b_uv = row3(sgu_b_uv)
    sgu_nw = row3(sgu_norm)
    w_s = sgu_w_s.astype(BF16)
    b_s = jnp.broadcast_to(sgu_b_s[..., None], sgu_b_s.shape + (LANES,))

    w_uv = None
    for i in range(depth):
        xs, ffn_w = ffn(xs, 1, i, ffn_w)
        j = i // 2
        if i % 2 == 0:
            zx, u, dt_all, w_out = _inproj(xs, mixn, w_in, w_in_u, w_in_dt, (even_w_out, j), i, j)
            xbc = _conv(zx, ssd_conv_w, conv_b, j)
            y_ssd = _ssd(xbc, zx, dt_all, par, dskip, ssd_nw, j)
            fa, fb = _fourier_weights(fourier_w, j)
            y_fft = _dft_b(_dft_a(u), fa, fb, seq)
            nxt = (sgu_w_uv, j) if i + 1 < depth else None
            xs, w_uv = _outproj(xs, [y_ssd, y_fft], w_out, cast=nxt)
        else:
            gated, w_out = _sgu(xs, mixn, w_uv, b_uv, sgu_nw, w_s, b_s, (odd_w_out, j), i, j)
            xs, _ = _outproj(xs, [gated], w_out)
        xs, ffn_w = ffn(xs, 2, i, ffn_w)
    return xs.reshape(bsz, seq, d)
```

```python
import functools

import numpy as np
import jax
import jax.numpy as jnp
from jax import lax
from jax.experimental import pallas as pl
from jax.experimental.pallas import tpu as pltpu

F32 = jnp.float32
BF16 = jnp.bfloat16
HIGHEST = lax.Precision.HIGHEST
LOG2E = 1.4426950408889634

D_MODEL = 2048
D_FF = 5632
EPS = 1e-6
SSD_HEAD_DIM = 64
SSD_HEADS = 48
SSD_GROUPS = 8
SSD_RANK = SSD_HEADS // SSD_GROUPS
SSD_GW = SSD_RANK * SSD_HEAD_DIM
SSD_STATE = 128
SSD_CHUNK = 256
SSD_WIDTH = SSD_HEADS * SSD_HEAD_DIM
SSD_CONV = 5
SSD_CONV_CH = SSD_WIDTH + 2 * SSD_GROUPS * SSD_STATE
FOURIER_WIDTH = 1024
FOURIER_GROUPS = 4
FOURIER_GD = FOURIER_WIDTH // FOURIER_GROUPS
SGU_WIDTH = 4096
SGU_GROUPS = 8
SGU_GD = SGU_WIDTH // SGU_GROUPS
SGU_CHUNK = 128
ZX_WIDTH = SSD_WIDTH + SSD_CONV_CH

LANES = 128
SUBLANES = 8
VMEM_LIMIT = 56 * 1024 * 1024
VMEM_LIMIT_BIG = 60 * 1024 * 1024

TM = 512
TMX = 1024
TF = 512
TN = 1024
SSD_GPS = 2
DT_SLOT = 16
DT_TILE = 256
assert SSD_GROUPS * DT_SLOT <= LANES and 2 * SSD_RANK <= DT_SLOT
DFT_N1 = 128
DFT_KB = 8


def _cparams(*sem, vmem_limit=VMEM_LIMIT):
    return pltpu.CompilerParams(dimension_semantics=sem, vmem_limit_bytes=vmem_limit)


def _rmsnorm(x, w):
    ms = jnp.mean(x * x, axis=-1, keepdims=True)
    return x * lax.rsqrt(ms + EPS) * w


def _ffn_body(*refs, cast_next, out_norm):
    x_ref, nw_ref, wg_ref, wu_ref, wd_ref = refs[:5]
    pos = 5
    nxt_in = refs[pos:pos + 3] if cast_next else ()
    pos += len(nxt_in)
    out_nw_ref = refs[pos] if out_norm else None
    pos += int(out_norm)
    o_ref = refs[pos]
    nxt_out = refs[pos + 1:pos + 1 + len(nxt_in)]
    xn_ref = refs[-1]
    j = pl.program_id(1)

    def half_swiglu():
        xn = xn_ref[...]
        g = jnp.dot(xn, wg_ref[...], preferred_element_type=F32)
        u = jnp.dot(xn, wu_ref[...], preferred_element_type=F32)
        h = ((0.5 * g) * jax.nn.sigmoid(g) * u).astype(BF16)
        return jnp.dot(h, wd_ref[...], preferred_element_type=F32)

    def cast_weight_blocks():
        for src, dst in zip(nxt_in, nxt_out):
            dst[...] = src[...].astype(BF16)

    @pl.when(j == 0)
    def _():
        xn_ref[...] = _rmsnorm(x_ref[...], nw_ref[...]).astype(BF16)
        o_ref[...] = x_ref[...] + half_swiglu()
        cast_weight_blocks()

    @pl.when(j > 0)
    def _():
        o_ref[...] += half_swiglu()
        cast_weight_blocks()

    if out_norm:
        @pl.when(j == pl.num_programs(1) - 1)
        def _():
            o_ref[...] = _rmsnorm(o_ref[...], out_nw_ref[...])


def _ffn(x, nw, w, layer, next_w=None, out_norm=None):
    seq = x.shape[0]
    tm = min(TMX, seq)
    ni, nj = seq // tm, D_FF // TF
    in_specs = [
        pl.BlockSpec((tm, D_MODEL), lambda i, j: (i, 0)),
        pl.BlockSpec((None, 1, D_MODEL), lambda i, j: (layer, 0, 0)),
        pl.BlockSpec((D_MODEL, TF), lambda i, j: (0, j)),
        pl.BlockSpec((D_MODEL, TF), lambda i, j: (0, j)),
        pl.BlockSpec((TF, D_MODEL), lambda i, j: (j, 0)),
    ]
    operands = [x, nw, *w]
    out_specs = [pl.BlockSpec((tm, D_MODEL), lambda i, j: (i, 0))]
    out_shape = [jax.ShapeDtypeStruct((seq, D_MODEL), F32)]
    if next_w is not None:
        g32, u32, d32, nl = next_w
        assert D_MODEL % ni == 0
        rb = D_MODEL // ni
        in_specs += [pl.BlockSpec((None, rb, TF), lambda i, j: (nl, i, j)),
                     pl.BlockSpec((None, rb, TF), lambda i, j: (nl, i, j)),
                     pl.BlockSpec((None, TF, rb), lambda i, j: (nl, j, i))]
        operands += [g32, u32, d32]
        out_specs += [pl.BlockSpec((rb, TF), lambda i, j: (i, j)),
                      pl.BlockSpec((rb, TF), lambda i, j: (i, j)),
                      pl.BlockSpec((TF, rb), lambda i, j: (j, i))]
        out_shape += [jax.ShapeDtypeStruct((D_MODEL, D_FF), BF16),
                      jax.ShapeDtypeStruct((D_MODEL, D_FF), BF16),
                      jax.ShapeDtypeStruct((D_FF, D_MODEL), BF16)]
    if out_norm is not None:
        in_specs.append(pl.BlockSpec((1, D_MODEL), lambda i, j: (0, 0)))
        operands.append(out_norm)
    outs = pl.pallas_call(
        functools.partial(_ffn_body, cast_next=next_w is not None, out_norm=out_norm is not None),
        grid=(ni, nj),
        in_specs=in_specs,
        out_specs=out_specs,
        out_shape=out_shape,
        scratch_shapes=[pltpu.VMEM((tm, D_MODEL), BF16)],
        compiler_params=_cparams("parallel", "arbitrary", vmem_limit=VMEM_LIMIT_BIG),
        name="ffn",
    )(*operands)
    return outs[0], tuple(outs[1:])


N_ZX_TILES = ZX_WIDTH // TN
IN_TILES = N_ZX_TILES + 2


def _inproj_body(x_ref, nw_ref, w_ref, wu_ref, wdt_ref, cast_src, zx_ref, u_ref, dt_ref, cast_dst, h_ref):
    j = pl.program_id(1)
    cast_dst[...] = cast_src[...].astype(BF16)

    @pl.when(j == 0)
    def _():
        h_ref[...] = _rmsnorm(x_ref[...], nw_ref[...]).astype(BF16)

    @pl.when(j < N_ZX_TILES)
    def _():
        zx_ref[...] = lax.dot_general(h_ref[...], w_ref[...], (((1,), (1,)), ((), ())),
                                      preferred_element_type=F32).astype(BF16)

    @pl.when(j == N_ZX_TILES)
    def _():
        u_ref[...] = lax.dot_general(h_ref[...], wu_ref[...], (((1,), (1,)), ((), ())),
                                     preferred_element_type=F32).astype(BF16)

    @pl.when(j == N_ZX_TILES + 1)
    def _():
        dt_ref[...] = lax.dot_general(h_ref[...], wdt_ref[...], (((1,), (1,)), ((), ())),
                                      preferred_element_type=F32)


def _inproj(x, nw, w, w_u, w_dt, cast, layer, j_even):
    seq = x.shape[0]
    tm = min(TMX, seq)
    ni = seq // tm
    once = pl.Buffered(1)
    c_in, c_out, c_shape = _cast_specs(cast, ni, N_ZX_TILES, lambda i, j: (i, jnp.minimum(j, N_ZX_TILES - 1)))
    return pl.pallas_call(
        _inproj_body,
        grid=(ni, IN_TILES),
        in_specs=[
            pl.BlockSpec((tm, D_MODEL), lambda i, j: (i, 0)),
            pl.BlockSpec((None, 1, D_MODEL), lambda i, j: (layer, 0, 0)),
            pl.BlockSpec((None, TN, D_MODEL), lambda i, j: (j_even, jnp.minimum(j, N_ZX_TILES - 1), 0)),
            pl.BlockSpec((None, FOURIER_WIDTH, D_MODEL), lambda i, j: (j_even, 0, 0), pipeline_mode=once),
            pl.BlockSpec((None, DT_TILE, D_MODEL), lambda i, j: (j_even, 0, 0), pipeline_mode=once),
            c_in,
        ],
        out_specs=[
            pl.BlockSpec((tm, TN), lambda i, j: (i, jnp.minimum(j, N_ZX_TILES - 1))),
            pl.BlockSpec((tm, FOURIER_WIDTH), lambda i, j: (i, 0)),
            pl.BlockSpec((tm, DT_TILE), lambda i, j: (i, 0)),
            c_out,
        ],
        out_shape=[
            jax.ShapeDtypeStruct((seq, ZX_WIDTH), BF16),
            jax.ShapeDtypeStruct((seq, FOURIER_WIDTH), BF16),
            jax.ShapeDtypeStruct((seq, DT_TILE), F32),
            c_shape,
        ],
        scratch_shapes=[pltpu.VMEM((tm, D_MODEL), BF16)],
        compiler_params=_cparams("parallel", "arbitrary", vmem_limit=VMEM_LIMIT_BIG),
        name="even_inproj",
    )(x, nw, w, w_u, w_dt, cast[0])


CONV_TR = 2048
CONV_TC = 512
CONV_HALO = 16


def _conv_body(xm_ref, xp_ref, xn_ref, w_ref, b_ref, o_ref):
    i = pl.program_id(0)
    last = pl.num_programs(0) - 1
    tr = xm_ref.shape[0]
    prev = xp_ref[...].astype(F32)[CONV_HALO - SUBLANES:]
    nxt = xn_ref[...].astype(F32)[:SUBLANES]
    ext = jnp.concatenate([jnp.where(i == 0, 0.0, prev), xm_ref[...].astype(F32), jnp.where(i == last, 0.0, nxt)],
                          axis=0)
    n = tr + 2 * SUBLANES
    w = w_ref[...]
    acc = jnp.broadcast_to(b_ref[...], o_ref.shape)
    half = SSD_CONV // 2
    for k in range(SSD_CONV):
        shifted = ext if k == half else pltpu.roll(ext, (half - k) % n, axis=0)
        acc = acc + shifted[SUBLANES:SUBLANES + tr, :] * w[k:k + 1, :]
    o_ref[...] = (acc * jax.nn.sigmoid(acc)).astype(BF16)


def _conv(zx, conv_w, conv_b, layer):
    seq = zx.shape[0]
    tr = min(CONV_TR, seq)
    col0 = SSD_WIDTH // CONV_TC
    hb = tr // CONV_HALO
    nhb = seq // CONV_HALO
    return pl.pallas_call(
        _conv_body,
        grid=(seq // tr, SSD_CONV_CH // CONV_TC),
        in_specs=[
            pl.BlockSpec((tr, CONV_TC), lambda i, j: (i, col0 + j)),
            pl.BlockSpec((CONV_HALO, CONV_TC), lambda i, j: (jnp.maximum(i * hb - 1, 0), col0 + j)),
            pl.BlockSpec((CONV_HALO, CONV_TC), lambda i, j: (jnp.minimum((i + 1) * hb, nhb - 1), col0 + j)),
            pl.BlockSpec((None, SSD_CONV, CONV_TC), lambda i, j: (layer, 0, j)),
            pl.BlockSpec((None, 1, CONV_TC), lambda i, j: (layer, 0, j)),
        ],
        out_specs=pl.BlockSpec((tr, CONV_TC), lambda i, j: (i, j)),
        out_shape=jax.ShapeDtypeStruct((seq, SSD_CONV_CH), BF16),
        compiler_params=_cparams("parallel", "parallel"),
        name="ssd_conv",
    )(zx, zx, zx, conv_w, conv_b)


def _softplus(v):
    return jnp.maximum(v, 0.0) + jnp.log1p(jnp.exp(-jnp.abs(v)))


def _head_rows(rows, first):
    n = rows.shape[1]
    return jnp.concatenate([jnp.broadcast_to(rows[first + h:first + h + 1, :], (SSD_HEAD_DIM, n))
                            for h in range(SSD_RANK)], axis=0)


def _ssd_body(x_ref, b_ref, c_ref, z_ref, dt_ref, par_ref, dskip_ref, nw_ref, tri_ref, o_ref, *scratch):
    phase = pl.program_id(1)
    c = pl.program_id(2)
    nc = pl.num_programs(2)
    sf_ref, sb_ref = scratch[:2]
    first_group = pl.program_id(0) * SSD_GPS

    def group(gi):
        cols = lambda w: slice(gi * w, (gi + 1) * w)
        return dict(x_ref=x_ref.at[:, cols(SSD_GW)], b_ref=b_ref.at[:, cols(SSD_STATE)],
                    c_ref=c_ref.at[:, cols(SSD_STATE)], z_ref=z_ref.at[:, cols(SSD_GW)],
                    dt_ref=dt_ref, group_index=first_group + gi, par_ref=par_ref.at[gi], dskip_ref=dskip_ref.at[gi],
                    nw_ref=nw_ref.at[gi], tri_ref=tri_ref, o_ref=o_ref.at[:, cols(SSD_GW)],
                    scratch=[s.at[gi] for s in scratch])

    @pl.when(phase == 0)
    def _():
        @pl.when(c == 0)
        def _():
            sb_ref[...] = jnp.zeros_like(sb_ref)

        for gi in range(SSD_GPS):
            _ssd_prepare(nc - 1 - c, **group(gi))

    @pl.when(phase == 1)
    def _():
        @pl.when(c == 0)
        def _():
            sf_ref[...] = jnp.zeros_like(sf_ref)

        for gi in range(SSD_GPS):
            _ssd_emit(c, **group(gi))


def _ssd_prepare(cc, group_index, x_ref, b_ref, dt_ref, par_ref, tri_ref, scratch, **unused):
    _, sb_ref, sball_ref, xt_ref, qn_ref, dtt_ref, qt_ref, tott_ref = scratch
    t = SSD_CHUNK
    rk = SSD_RANK
    prm = dtt_ref.shape[1]
    bm = b_ref[...]
    par = par_ref[...]
    lane = lax.broadcasted_iota(jnp.int32, (1, LANES), 1)
    dt_raw = pltpu.roll(dt_ref[:, :LANES], (LANES - DT_SLOT * group_index) % LANES, axis=1)
    dt = _softplus(dt_raw + par[0:1, :])
    da = dt * (-jnp.exp(par[1:2, :]))
    d1 = da.astype(BF16)
    r1 = da - d1.astype(F32)
    d2 = r1.astype(BF16)
    d3 = (r1 - d2.astype(F32)).astype(BF16)
    cs3 = jnp.dot(tri_ref[...], jnp.concatenate([d1, d2, d3], axis=1), preferred_element_type=F32)
    cs = cs3[:, :LANES] + cs3[:, LANES:2 * LANES] + cs3[:, 2 * LANES:]
    q = (cs - jnp.where(lane >= rk, da, 0.0)) * LOG2E
    dt_t = dt.T[:prm]
    q_t = q.T[:prm]
    tot_t = jnp.broadcast_to((cs * LOG2E).T[:prm, t - 1:t], (prm, t))
    x_t = x_ref[...].astype(F32).T
    xt_ref[cc] = x_t
    qn_ref[cc] = q
    dtt_ref[cc] = dt_t
    qt_ref[cc] = q_t
    tott_ref[cc] = tot_t

    sb = sb_ref[...]
    sball_ref[cc] = sb.astype(BF16)
    wb = dt_t * jnp.exp2(q_t)
    xw = (x_t * _head_rows(wb, rk)).astype(BF16)
    upd = jnp.dot(xw, bm, preferred_element_type=F32)
    sb_ref[...] = _head_rows(jnp.exp2(tot_t[:, :SSD_STATE]), rk) * sb + upd


def _ssd_emit(c, x_ref, b_ref, c_ref, z_ref, dskip_ref, nw_ref, o_ref, scratch, **unused):
    sf_ref, _, sball_ref, xt_ref, qn_ref, dtt_ref, qt_ref, tott_ref = scratch
    t = SSD_CHUNK
    rk = SSD_RANK
    hd = SSD_HEAD_DIM
    bm = b_ref[...]
    x_t = xt_ref[c]
    q = qn_ref[c]
    dt_t = dtt_ref[c]
    q_t = qt_ref[c]
    tot_t = tott_ref[c]
    cm = c_ref[...]
    gt = lax.dot_general(bm, cm, (((1,), (1,)), ((), ())), preferred_element_type=F32)
    hf = t // 2
    srow = lax.broadcasted_iota(jnp.int32, (hf, hf), 0)
    tcol = lax.broadcasted_iota(jnp.int32, (hf, hf), 1)
    causal = srow <= tcol
    anti = srow >= tcol
    zero = jnp.zeros((hf, hf), BF16)
    g00, g01, g10, g11 = gt[:hf, :hf], gt[:hf, hf:], gt[hf:, :hf], gt[hf:, hf:]
    ys = []
    for h in range(rk):
        hb = rk + h
        xh = x_t[h * hd:(h + 1) * hd, :]
        lhs = jnp.concatenate([(xh * dt_t[h:h + 1, :]).astype(BF16), (xh * dt_t[hb:hb + 1, :]).astype(BF16)],
                              axis=1)
        qs, qt = q[:, h:h + 1], q_t[h:h + 1, :]
        f00 = (g00 * jnp.where(causal, jnp.exp2(qt[:, :hf] - qs[:hf]), 0.0)).astype(BF16)
        f01 = (g01 * jnp.exp2(qt[:, hf:] - qs[:hf])).astype(BF16)
        f11 = (g11 * jnp.where(causal, jnp.exp2(qt[:, hf:] - qs[hf:]), 0.0)).astype(BF16)
        qs, qt = q[:, hb:hb + 1], q_t[hb:hb + 1, :]
        b00 = (g00 * jnp.where(anti, jnp.exp2(qs[:hf] - qt[:, :hf]), 0.0)).astype(BF16)
        b10 = (g10 * jnp.exp2(qs[hf:] - qt[:, :hf])).astype(BF16)
        b11 = (g11 * jnp.where(anti, jnp.exp2(qs[hf:] - qt[:, hf:]), 0.0)).astype(BF16)
        rhs = jnp.concatenate([jnp.concatenate([f00, f01], axis=1), jnp.concatenate([zero, f11], axis=1),
                               jnp.concatenate([b00, zero], axis=1), jnp.concatenate([b10, b11], axis=1)],
                              axis=0)
        ys.append(jnp.dot(lhs, rhs, preferred_element_type=F32))
    y_t = jnp.concatenate(ys, axis=0)

    sf = sf_ref[...]
    states = jnp.concatenate([sf.astype(BF16), sball_ref[c]], axis=0)
    off = lax.dot_general(states, cm, (((1,), (1,)), ((), ())), preferred_element_type=F32)
    y_t += off[:SSD_GW] * _head_rows(jnp.exp2(q_t), 0)
    y_t += off[SSD_GW:] * _head_rows(jnp.exp2(tot_t - q_t), rk)
    wf = dt_t * jnp.exp2(tot_t - q_t)
    xw = (x_t * _head_rows(wf, 0)).astype(BF16)
    upd = jnp.dot(xw, bm, preferred_element_type=F32)
    sf_ref[...] = _head_rows(jnp.exp2(tot_t[:, :SSD_STATE]), 0) * sf + upd

    y = y_t.T + dskip_ref[...] * x_ref[...].astype(F32)
    z = z_ref[...].astype(F32)
    y = y * (z * jax.nn.sigmoid(z))
    o_ref[...] = _rmsnorm(y, nw_ref[...]).astype(BF16)


def _ssd(xbc, zx, dt_all, par, dskip, nw, layer):
    seq = xbc.shape[0]
    nc = seq // SSD_CHUNK
    t = SSD_CHUNK
    b0 = SSD_WIDTH // SSD_STATE
    c0 = b0 + SSD_GROUPS
    tri = jnp.asarray(np.tril(np.ones((t, t), np.float32)), dtype=BF16)

    def cidx(p, c):
        return p * c + (1 - p) * (nc - 1 - c)

    gps = SSD_GPS
    assert b0 % gps == 0 and c0 % gps == 0
    return pl.pallas_call(
        _ssd_body,
        grid=(SSD_GROUPS // gps, 2, nc),
        in_specs=[
            pl.BlockSpec((t, gps * SSD_GW), lambda g, p, c: (cidx(p, c), g)),
            pl.BlockSpec((t, gps * SSD_STATE), lambda g, p, c: (cidx(p, c), b0 // gps + g)),
            pl.BlockSpec((t, gps * SSD_STATE), lambda g, p, c: (p * c, c0 // gps + g)),
            pl.BlockSpec((t, gps * SSD_GW), lambda g, p, c: (p * c, g)),
            pl.BlockSpec((t, DT_TILE), lambda g, p, c: ((1 - p) * (nc - 1 - c), 0)),
            pl.BlockSpec((None, gps, SUBLANES, LANES), lambda g, p, c: (layer, g, 0, 0)),
            pl.BlockSpec((None, gps, 1, SSD_GW), lambda g, p, c: (layer, g, 0, 0)),
            pl.BlockSpec((None, gps, 1, SSD_GW), lambda g, p, c: (layer, g, 0, 0)),
            pl.BlockSpec((t, t), lambda g, p, c: (0, 0)),
        ],
        out_specs=pl.BlockSpec((t, gps * SSD_GW), lambda g, p, c: (p * c, g)),
        out_shape=jax.ShapeDtypeStruct((seq, SSD_WIDTH), BF16),
        scratch_shapes=[
            pltpu.VMEM((gps, SSD_GW, SSD_STATE), F32),
            pltpu.VMEM((gps, SSD_GW, SSD_STATE), F32),
            pltpu.VMEM((gps, nc, SSD_GW, SSD_STATE), BF16),
            pltpu.VMEM((gps, nc, SSD_GW, t), F32),
            pltpu.VMEM((gps, nc, t, LANES), F32),
            pltpu.VMEM((gps, nc, 2 * SUBLANES, t), F32),
            pltpu.VMEM((gps, nc, 2 * SUBLANES, t), F32),
            pltpu.VMEM((gps, nc, 2 * SUBLANES, t), F32),
        ],
        compiler_params=_cparams("arbitrary", "arbitrary", "arbitrary"),
        name="ssd_scan",
    )(xbc, xbc, xbc, zx, dt_all, par, dskip, nw, tri)


def _fw_body(cd_ref, sd_ref, w_ref, a_ref, b_ref):
    w = w_ref[...]
    a_ref[...] = jnp.dot(cd_ref[...], w, preferred_element_type=F32, precision=HIGHEST).astype(BF16)
    b_ref[...] = jnp.dot(sd_ref[...], w, preferred_element_type=F32, precision=HIGHEST).astype(BF16)


def _fourier_weights(fourier_w, layer):
    d = FOURIER_GD
    ang = 2.0 * np.pi * np.outer(np.arange(d), np.arange(d)) / d
    cd = jnp.asarray((np.cos(ang) / np.sqrt(d)).astype(np.float32))
    sd = jnp.asarray((np.sin(ang) / np.sqrt(d)).astype(np.float32))
    return pl.pallas_call(
        _fw_body,
        grid=(FOURIER_GROUPS,),
        in_specs=[
            pl.BlockSpec((d, d), lambda g: (0, 0)),
            pl.BlockSpec((d, d), lambda g: (0, 0)),
            pl.BlockSpec((None, None, d, d), lambda g: (layer, g, 0, 0)),
        ],
        out_specs=[pl.BlockSpec((None, d, d), lambda g: (g, 0, 0))] * 2,
        out_shape=[jax.ShapeDtypeStruct((FOURIER_GROUPS, d, d), BF16)] * 2,
        compiler_params=_cparams("parallel"),
        name="fourier_weights",
    )(cd, sd, fourier_w)


DFT_NB = 4


def _dft_a_body(x_ref, f_ref, tc_ref, ts_ref, o_ref):
    n1 = DFT_N1
    y = jnp.dot(f_ref[...], x_ref[...], preferred_element_type=F32)
    reps = FOURIER_WIDTH // LANES
    for b in range(tc_ref.shape[0]):
        sl = slice(b * FOURIER_WIDTH, (b + 1) * FOURIER_WIDTH)
        yr = y[:n1, sl]
        yi = y[n1:, sl]
        tc = jnp.tile(tc_ref[b], (1, reps))
        ts = jnp.tile(ts_ref[b], (1, reps))
        o_ref[:n1, sl] = (yr * tc + yi * ts).astype(BF16)
        o_ref[n1:, sl] = (yi * tc - yr * ts).astype(BF16)


def _dft_a(u):
    seq = u.shape[0]
    n1 = DFT_N1
    n2 = seq // n1
    nb = min(DFT_NB, n2)
    ang1 = 2.0 * np.pi * np.outer(np.arange(n1), np.arange(n1)) / n1
    f1 = jnp.asarray(np.concatenate([np.cos(ang1), -np.sin(ang1)], axis=0), dtype=BF16)
    angt = 2.0 * np.pi * np.outer(np.arange(n2), np.arange(n1)) / seq
    tc = jnp.asarray(np.repeat(np.cos(angt)[:, :, None], LANES, axis=2).astype(np.float32))
    ts = jnp.asarray(np.repeat(np.sin(angt)[:, :, None], LANES, axis=2).astype(np.float32))
    x2 = u.reshape(n1, n2 * FOURIER_WIDTH)
    return pl.pallas_call(
        _dft_a_body,
        grid=(n2 // nb,),
        in_specs=[
            pl.BlockSpec((n1, nb * FOURIER_WIDTH), lambda j: (0, j)),
            pl.BlockSpec((2 * n1, n1), lambda j: (0, 0)),
            pl.BlockSpec((nb, n1, LANES), lambda j: (j, 0, 0)),
            pl.BlockSpec((nb, n1, LANES), lambda j: (j, 0, 0)),
        ],
        out_specs=pl.BlockSpec((2 * n1, nb * FOURIER_WIDTH), lambda j: (0, j)),
        out_shape=jax.ShapeDtypeStruct((2 * n1, n2 * FOURIER_WIDTH), BF16),
        compiler_params=_cparams("parallel"),
        name="dft_stage_a",
    )(x2, f1, tc, ts)


def _dft_b_body(yr_ref, yi_ref, lr_ref, li_ref, a_ref, b_ref, o_ref):
    kb, n2, width = yr_ref.shape
    rhs = jnp.concatenate([yr_ref[...].reshape(kb * n2, width), yi_ref[...].reshape(kb * n2, width)], axis=0)
    zr = jnp.dot(lr_ref[...], rhs, preferred_element_type=F32).astype(BF16)
    zi = jnp.dot(li_ref[...], rhs, preferred_element_type=F32).astype(BF16)
    outs = []
    for g in range(FOURIER_GROUPS):
        sl = slice(g * FOURIER_GD, (g + 1) * FOURIER_GD)
        outs.append(jnp.dot(zr[:, sl], a_ref[g], preferred_element_type=F32)
                    + jnp.dot(zi[:, sl], b_ref[g], preferred_element_type=F32))
    o_ref[...] = jnp.concatenate(outs, axis=1).reshape(o_ref.shape)


def _dft_b(ya, fa, fb, seq):
    n1 = DFT_N1
    n2 = seq // n1
    kb = DFT_KB
    ang2 = 2.0 * np.pi * np.outer(np.arange(n2), np.arange(n2)) / n2
    c2 = np.cos(ang2) / np.sqrt(seq)
    s2 = np.sin(ang2) / np.sqrt(seq)
    eye = np.eye(kb)
    lr = np.concatenate([np.einsum('ab,kn->kabn', eye, c2).reshape(n2 * kb, kb * n2),
                         np.einsum('ab,kn->kabn', eye, s2).reshape(n2 * kb, kb * n2)], axis=1)
    li = np.concatenate([np.einsum('ab,kn->kabn', eye, -s2).reshape(n2 * kb, kb * n2),
                         np.einsum('ab,kn->kabn', eye, c2).reshape(n2 * kb, kb * n2)], axis=1)
    y3 = ya.reshape(2 * n1, n2, FOURIER_WIDTH)
    nk = n1 // kb
    out = pl.pallas_call(
        _dft_b_body,
        grid=(nk,),
        in_specs=[
            pl.BlockSpec((kb, n2, FOURIER_WIDTH), lambda j: (j, 0, 0)),
            pl.BlockSpec((kb, n2, FOURIER_WIDTH), lambda j: (nk + j, 0, 0)),
            pl.BlockSpec((n2 * kb, 2 * kb * n2), lambda j: (0, 0)),
            pl.BlockSpec((n2 * kb, 2 * kb * n2), lambda j: (0, 0)),
            pl.BlockSpec((FOURIER_GROUPS, FOURIER_GD, FOURIER_GD), lambda j: (0, 0, 0)),
            pl.BlockSpec((FOURIER_GROUPS, FOURIER_GD, FOURIER_GD), lambda j: (0, 0, 0)),
        ],
        out_specs=pl.BlockSpec((n2, kb, FOURIER_WIDTH), lambda j: (0, j, 0)),
        out_shape=jax.ShapeDtypeStruct((n2, n1, FOURIER_WIDTH), F32),
        compiler_params=_cparams("parallel"),
        name="dft_stage_b",
    )(y3, y3, jnp.asarray(lr, dtype=BF16), jnp.asarray(li, dtype=BF16), fa, fb)
    return out.reshape(seq, FOURIER_WIDTH)


def _cast_specs(cast, n_rows, n_cols, block_index):
    src, layer = cast
    _, r, c = src.shape
    assert r % n_rows == 0 and c % n_cols == 0
    blk = (r // n_rows, c // n_cols)
    in_spec = pl.BlockSpec((None,) + blk, lambda *g: (layer,) + tuple(block_index(*g)))
    out_spec = pl.BlockSpec(blk, lambda *g: tuple(block_index(*g)))
    return in_spec, out_spec, jax.ShapeDtypeStruct((r, c), BF16)


def _outproj_body(*refs, n_lhs, cast):
    x_ref = refs[0]
    lhs = refs[1:1 + n_lhs]
    ws = refs[1 + n_lhs:1 + 2 * n_lhs]
    o_ref = refs[1 + 2 * n_lhs + int(cast)]
    acc = x_ref[...]
    for a_ref, w_ref in zip(lhs, ws):
        acc = acc + jnp.dot(a_ref[...].astype(BF16), w_ref[...], preferred_element_type=F32)
    o_ref[...] = acc
    if cast:
        refs[-1][...] = refs[1 + 2 * n_lhs][...].astype(BF16)


def _outproj(x, lhs_list, w, cast=None):
    seq = x.shape[0]
    nj, ni = D_MODEL // TN, seq // TM
    in_specs = [pl.BlockSpec((TM, TN), lambda j, i: (i, j))]
    for a in lhs_list:
        in_specs.append(pl.BlockSpec((TM, a.shape[1]), lambda j, i: (i, 0)))
    row = 0
    for a in lhs_list:
        k = a.shape[1]
        assert row % k == 0
        in_specs.append(pl.BlockSpec((k, TN), lambda j, i, rb=row // k: (rb, j)))
        row += k
    operands = [x, *lhs_list, *([w] * len(lhs_list))]
    out_specs = [pl.BlockSpec((TM, TN), lambda j, i: (i, j))]
    out_shape = [jax.ShapeDtypeStruct((seq, D_MODEL), F32)]
    if cast is not None:
        c_in, c_out, c_shape = _cast_specs(cast, ni, nj, lambda j, i: (i, j))
        in_specs.append(c_in)
        operands.append(cast[0])
        out_specs.append(c_out)
        out_shape.append(c_shape)
    outs = pl.pallas_call(
        functools.partial(_outproj_body, n_lhs=len(lhs_list), cast=cast is not None),
        grid=(nj, ni),
        in_specs=in_specs,
        out_specs=out_specs,
        out_shape=out_shape,
        compiler_params=_cparams("parallel", "parallel"),
        name="outproj",
    )(*operands)
    return outs[0], (outs[1] if cast is not None else None)


SGU_V_TILES = SGU_WIDTH // TN
SGU_TILES = 2 * SGU_V_TILES


def _sgu_body(x_ref, nw_ref, w_ref, b_ref, vnw_ref, ws_ref, bs_ref, cast_src, o_ref, cast_dst, h_ref, v_ref, ss_ref):
    j = pl.program_id(1)
    tm = x_ref.shape[0]
    cast_dst[...] = cast_src[...].astype(BF16)

    @pl.when(j == 0)
    def _():
        h_ref[...] = _rmsnorm(x_ref[...], nw_ref[...]).astype(BF16)
        ss_ref[...] = jnp.zeros_like(ss_ref)

    def project():
        return jax.nn.gelu(jnp.dot(h_ref[...], w_ref[...], preferred_element_type=F32) + b_ref[...])

    @pl.when(j < SGU_V_TILES)
    def _():
        act = project()
        v_ref[j] = act.astype(BF16)
        ss_ref[...] += jnp.sum(act * act, axis=-1, keepdims=True)

    @pl.when(j == SGU_V_TILES - 1)
    def _():
        rs = lax.rsqrt(ss_ref[...] * (1.0 / SGU_WIDTH) + EPS)
        per_tile = TN // SGU_GD
        for g in range(SGU_GROUPS):
            tile, off = divmod(g, per_tile)
            sl = slice(off * SGU_GD, (off + 1) * SGU_GD)
            v = v_ref[tile, :, sl].astype(F32)
            v = (v * rs * vnw_ref[:, g * SGU_GD:(g + 1) * SGU_GD]).astype(BF16)
            bias = jnp.tile(bs_ref[g], (1, SGU_GD // LANES))
            for qc in range(tm // SGU_CHUNK):
                rows = slice(qc * SGU_CHUNK, (qc + 1) * SGU_CHUNK)
                mixed = jnp.dot(ws_ref[g], v[rows], preferred_element_type=F32) + bias
                v_ref[tile, rows, sl] = mixed.astype(BF16)

    @pl.when(j >= SGU_V_TILES)
    def _():
        o_ref[...] = (project() * v_ref[j - SGU_V_TILES].astype(F32)).astype(BF16)


def _sgu(x, nw, w_uv, b_uv, vnw, w_s, b_s, cast, layer, j_odd):
    seq = x.shape[0]
    tm = min(TMX, seq)
    ni = seq // tm
    wcol = lambda j: (j + SGU_V_TILES) % SGU_TILES
    c_in, c_out, c_shape = _cast_specs(cast, ni, SGU_TILES, lambda i, j: (i, j))
    return pl.pallas_call(
        _sgu_body,
        grid=(ni, SGU_TILES),
        in_specs=[
            pl.BlockSpec((tm, D_MODEL), lambda i, j: (i, 0)),
            pl.BlockSpec((None, 1, D_MODEL), lambda i, j: (layer, 0, 0)),
            pl.BlockSpec((D_MODEL, TN), lambda i, j: (0, wcol(j))),
            pl.BlockSpec((None, 1, TN), lambda i, j: (j_odd, 0, wcol(j))),
            pl.BlockSpec((None, 1, SGU_WIDTH), lambda i, j: (j_odd, 0, 0)),
            pl.BlockSpec((None, SGU_GROUPS, SGU_CHUNK, SGU_CHUNK), lambda i, j: (j_odd, 0, 0, 0)),
            pl.BlockSpec((None, SGU_GROUPS, SGU_CHUNK, LANES), lambda i, j: (j_odd, 0, 0, 0)),
            c_in,
        ],
        out_specs=[pl.BlockSpec((tm, TN), lambda i, j: (i, jnp.maximum(j - SGU_V_TILES, 0))), c_out],
        out_shape=[jax.ShapeDtypeStruct((seq, SGU_WIDTH), BF16), c_shape],
        scratch_shapes=[
            pltpu.VMEM((tm, D_MODEL), BF16),
            pltpu.VMEM((SGU_V_TILES, tm, TN), BF16),
            pltpu.VMEM((tm, 1), F32),
        ],
        compiler_params=_cparams("parallel", "arbitrary"),
        name="sgu",
    )(x, nw, w_uv, b_uv, vnw, w_s, b_s, cast[0])


def _even_in_weights(w):
    n, d = w.shape[0], w.shape[1]
    wt = jnp.swapaxes(w, 1, 2).astype(BF16)
    dt0 = ZX_WIDTH
    u0 = dt0 + 2 * SSD_HEADS
    w_dt = wt[:, dt0:u0, :].reshape(n, 2, SSD_GROUPS, SSD_RANK, d)
    w_dt = jnp.transpose(w_dt, (0, 2, 1, 3, 4)).reshape(n, SSD_GROUPS, 2 * SSD_RANK, d)
    w_dt = jnp.pad(w_dt, ((0, 0), (0, 0), (0, DT_SLOT - 2 * SSD_RANK), (0, 0))).reshape(n, SSD_GROUPS * DT_SLOT, d)
    w_dt = jnp.pad(w_dt, ((0, 0), (0, DT_TILE - SSD_GROUPS * DT_SLOT), (0, 0)))
    return wt, wt[:, u0:, :], w_dt


def _group_lanes(p):
    n = p.shape[0]
    p = jnp.transpose(p.reshape(n, 2, SSD_GROUPS, SSD_RANK), (0, 2, 1, 3)).reshape(n, SSD_GROUPS, 2 * SSD_RANK)
    return jnp.pad(p, ((0, 0), (0, 0), (0, LANES - 2 * SSD_RANK)))


def kernel(x, ffn1_norm, ffn1_w_gate, ffn1_w_up, ffn1_w_down, mix_norm, ffn2_norm, ffn2_w_gate, ffn2_w_up,
           ffn2_w_down, even_w_in, ssd_conv_w, ssd_conv_b, ssd_dt_bias, ssd_a_log, ssd_d, ssd_norm, fourier_w,
           even_w_out, sgu_w_uv, sgu_b_uv, sgu_norm, sgu_w_s, sgu_b_s, odd_w_out, final_norm):
    bsz, seq, d = x.shape
    assert bsz == 1 and d == D_MODEL
    depth = ffn1_norm.shape[0]
    n_even = even_w_in.shape[0]
    xs = x.reshape(seq, d)

    row3 = lambda a: a.reshape(a.shape[0], 1, a.shape[1])
    assert depth >= 1
    ffn_f32 = {1: (ffn1_w_gate, ffn1_w_up, ffn1_w_down), 2: (ffn2_w_gate, ffn2_w_up, ffn2_w_down)}
    ffn_nw = {1: row3(ffn1_norm), 2: row3(ffn2_norm)}
    ffn_w = tuple(w[0].astype(BF16) for w in ffn_f32[1])

    def ffn(xs, which, i, ffn_w):
        last = which == 2 and i == depth - 1
        nxt = None if last else ((*ffn_f32[2], i) if which == 1 else (*ffn_f32[1], i + 1))
        return _ffn(xs, ffn_nw[which], ffn_w, i, next_w=nxt, out_norm=final_norm.reshape(1, d) if last else None)

    mixn = row3(mix_norm)
    w_in, w_in_u, w_in_dt = _even_in_weights(even_w_in)
    zeros = jnp.zeros((n_even, SSD_GROUPS, SUBLANES - 2, LANES), F32)
    par = jnp.concatenate([_group_lanes(ssd_dt_bias)[:, :, None, :], _group_lanes(ssd_a_log)[:, :, None, :], zeros],
                          axis=2)
    dskip = jnp.repeat(ssd_d, SSD_HEAD_DIM, axis=1).reshape(n_even, SSD_GROUPS, 1, SSD_GW)
    ssd_nw = ssd_norm.reshape(n_even, SSD_GROUPS, 1, SSD_GW)
    conv_b = row3(ssd_conv_b)
    b_uv = row3(sgu_b_uv)
    sgu_nw = row3(sgu_norm)
    w_s = sgu_w_s.astype(BF16)
    b_s = jnp.broadcast_to(sgu_b_s[..., None], sgu_b_s.shape + (LANES,))

    w_uv = None
    for i in range(depth):
        xs, ffn_w = ffn(xs, 1, i, ffn_w)
        j = i // 2
        if i % 2 == 0:
            zx, u, dt_all, w_out = _inproj(xs, mixn, w_in, w_in_u, w_in_dt, (even_w_out, j), i, j)
            xbc = _conv(zx, ssd_conv_w, conv_b, j)
            y_ssd = _ssd(xbc, zx, dt_all, par, dskip, ssd_nw, j)
            fa, fb = _fourier_weights(fourier_w, j)
            y_fft = _dft_b(_dft_a(u), fa, fb, seq)
            nxt = (sgu_w_uv, j) if i + 1 < depth else None
            xs, w_uv = _outproj(xs, [y_ssd, y_fft], w_out, cast=nxt)
        else:
            gated, w_out = _sgu(xs, mixn, w_uv, b_uv, sgu_nw, w_s, b_s, (odd_w_out, j), i, j)
            xs, _ = _outproj(xs, [gated], w_out)
        xs, ffn_w = ffn(xs, 2, i, ffn_w)
    return xs.reshape(bsz, seq, d)
```
